```python
import math
import jax, jax.numpy as jnp
from jax import lax
import numpy as np

D_MODEL = 1024
BATCH = 4
SEQ = 4096
DEPTH = 1
DEC_BATCH = 128
DEC_SEQ = 1
PAST_LEN = 8192
PAGE_SIZE = 128

MIX_WIDTH = D_MODEL
ATT_WIDTH = MIX_WIDTH // 2
SSD_WIDTH = MIX_WIDTH - ATT_WIDTH
HEAD_DIM = 64
N_ATT_HEADS = ATT_WIDTH // HEAD_DIM
SSD_HEAD_DIM = 64
N_SSD_HEADS = SSD_WIDTH // SSD_HEAD_DIM
N_SSD_GROUPS = 2
HEADS_PER_GROUP = N_SSD_HEADS // N_SSD_GROUPS
D_STATE = 128
GN = N_SSD_GROUPS * D_STATE
CONV_WIDTH = 4
CONV_DIM = SSD_WIDTH + 2 * GN
IN_WIDTH = 3 * ATT_WIDTH + N_ATT_HEADS + SSD_WIDTH + CONV_DIM + N_SSD_HEADS
FFN_HIDDEN = ((8 * D_MODEL + 3 * 256 - 1) // (3 * 256)) * 256
QUERY_BLOCK = 128
SSD_CHUNK = 128
NORM_EPS = 1e-6
ATT_SCALE = HEAD_DIM ** -0.5

kernel_name = 'hymba_fox_ssd_adaln_step'


def rmsnorm(x, g):
    xf = x.astype(jnp.float32)
    y = xf * lax.rsqrt(jnp.mean(xf * xf, axis=-1, keepdims=True) + NORM_EPS)
    return y * g.astype(jnp.float32)


def modulate(x, g, shift, scale):
    return (rmsnorm(x, g) * (1.0 + scale[:, None]) + shift[:, None]).astype(x.dtype)


def ada_terms(c, w_ada, b_ada):
    mod = jax.nn.silu(c) @ w_ada + b_ada
    return jnp.split(mod, 6, axis=-1)


def project_in(h, w_in, b_f):
    bsz, seqlen = h.shape[0], h.shape[1]
    sizes = (ATT_WIDTH, ATT_WIDTH, ATT_WIDTH, N_ATT_HEADS, SSD_WIDTH, CONV_DIM, N_SSD_HEADS)
    cuts = [int(v) for v in np.cumsum(sizes)[:-1]]
    q, k, v, f_raw, z, xbc, dt_raw = jnp.split(h @ w_in, cuts, axis=-1)
    shp = (bsz, seqlen, N_ATT_HEADS, HEAD_DIM)
    logf = jax.nn.log_sigmoid((f_raw + b_f).astype(jnp.float32))
    return q.reshape(shp), k.reshape(shp), v.reshape(shp), logf, z, xbc, dt_raw


def causal_conv(xpad, w, b):
    seqlen = xpad.shape[1] - (CONV_WIDTH - 1)
    acc = b
    for i in range(CONV_WIDTH):
        acc = acc + xpad[:, i:i + seqlen] * w[i]
    return jax.nn.silu(acc)


def ssd_inputs(xbc_c, dt_raw, dt_bias, A_log):
    bsz, seqlen = xbc_c.shape[0], xbc_c.shape[1]
    xs = xbc_c[..., :SSD_WIDTH].reshape(bsz, seqlen, N_SSD_GROUPS, HEADS_PER_GROUP, SSD_HEAD_DIM)
    Bm = xbc_c[..., SSD_WIDTH:SSD_WIDTH + GN].reshape(bsz, seqlen, N_SSD_GROUPS, D_STATE)
    Cm = xbc_c[..., SSD_WIDTH + GN:].reshape(bsz, seqlen, N_SSD_GROUPS, D_STATE)
    dt = jax.nn.softplus(dt_raw.astype(jnp.float32) + dt_bias.astype(jnp.float32))
    dt = dt.reshape(bsz, seqlen, N_SSD_GROUPS, HEADS_PER_GROUP)
    A = -jnp.exp(A_log.astype(jnp.float32)).reshape(N_SSD_GROUPS, HEADS_PER_GROUP)
    return xs, Bm, Cm, dt, dt * A


def fox_prompt(q, k, v, logf):
    bsz, seqlen = q.shape[0], q.shape[1]
    Ft = jnp.cumsum(logf, axis=1).transpose(0, 2, 1)
    kpos = jnp.arange(seqlen)
    n_blocks = seqlen // QUERY_BLOCK

    def block(i):
        start = i * QUERY_BLOCK
        qb = lax.dynamic_slice_in_dim(q, start, QUERY_BLOCK, axis=1)
        Fq = lax.dynamic_slice_in_dim(Ft, start, QUERY_BLOCK, axis=2)
        qpos = start + jnp.arange(QUERY_BLOCK)
        s = jnp.einsum('bqhd,bkhd->bhqk', qb, k).astype(jnp.float32) * ATT_SCALE
        s = s + Fq[..., :, None] - Ft[..., None, :]
        s = jnp.where(qpos[:, None] >= kpos[None, :], s, -jnp.inf)
        p = jax.nn.softmax(s, axis=-1).astype(v.dtype)
        return jnp.einsum('bhqk,bkhd->bqhd', p, v)

    o = lax.map(block, jnp.arange(n_blocks))
    return jnp.moveaxis(o, 0, 1).reshape(bsz, seqlen, ATT_WIDTH)


def fox_sample(q, k, v, logf, k_past, v_past, logf_past):
    bsz, tq = q.shape[0], q.shape[1]
    past = k_past.shape[1]
    lp = logf_past.astype(jnp.float32)
    suffix = lax.cumsum(lp, axis=1, reverse=True) - lp
    cn = jnp.cumsum(logf, axis=1).transpose(0, 2, 1)
    s_past = jnp.einsum('bqhd,bkhd->bhqk', q, k_past).astype(jnp.float32) * ATT_SCALE
    s_past = s_past + cn[..., :, None] + suffix.transpose(0, 2, 1)[..., None, :]
    s_new = jnp.einsum('bqhd,bkhd->bhqk', q, k).astype(jnp.float32) * ATT_SCALE
    s_new = s_new + cn[..., :, None] - cn[..., None, :]
    tri = jnp.tril(jnp.ones((tq, tq), dtype=bool))
    s_new = jnp.where(tri, s_new, -jnp.inf)
    p = jax.nn.softmax(jnp.concatenate([s_past, s_new], axis=-1), axis=-1)
    o = jnp.einsum('bhqk,bkhd->bqhd', p[..., :past].astype(v.dtype), v_past)
    o = o + jnp.einsum('bhqk,bkhd->bqhd', p[..., past:].astype(v.dtype), v)
    return o.reshape(bsz, tq, ATT_WIDTH)


def ssd_prompt(xs, Bm, Cm, dt, dA, D_skip):
    bsz, seqlen = xs.shape[0], xs.shape[1]
    nc = seqlen // SSD_CHUNK
    G, R, P, N = N_SSD_GROUPS, HEADS_PER_GROUP, SSD_HEAD_DIM, D_STATE
    xc = xs.reshape(bsz, nc, SSD_CHUNK, G, R, P)
    Bc = Bm.reshape(bsz, nc, SSD_CHUNK, G, N)
    Cc = Cm.reshape(bsz, nc, SSD_CHUNK, G, N)
    dtc = dt.reshape(bsz, nc, SSD_CHUNK, G, R)
    cs = jnp.cumsum(dA.reshape(bsz, nc, SSD_CHUNK, G, R), axis=2)
    seg = cs[:, :, :, None] - cs[:, :, None, :]
    tri = (jnp.arange(SSD_CHUNK)[:, None] >= jnp.arange(SSD_CHUNK)[None, :])[:, :, None, None]
    Lmat = jnp.exp(jnp.where(tri, seg, -jnp.inf))
    CB = jnp.einsum('bcign,bcjgn->bcijg', Cc, Bc)
    y_diag = jnp.einsum('bcijg,bcijgr,bcjgr,bcjgrp->bcigrp', CB, Lmat, dtc, xc)
    decay_states = jnp.exp(cs[:, :, -1:] - cs)
    states = jnp.einsum('bcjgn,bcjgr,bcjgrp->bcgrpn', Bc, decay_states * dtc, xc)
    chunk_decay = jnp.exp(cs[:, :, -1])

    def step(h, inp):
        dec, st = inp
        return dec[..., None, None] * h + st, h

    h0 = jnp.zeros((bsz, G, R, P, N), states.dtype)
    h_final, h_prev = lax.scan(step, h0, (jnp.moveaxis(chunk_decay, 1, 0), jnp.moveaxis(states, 1, 0)))
    h_prev = jnp.moveaxis(h_prev, 0, 1)
    y_off = jnp.einsum('bcign,bcgrpn,bcigr->bcigrp', Cc, h_prev, jnp.exp(cs))
    y = y_diag + y_off + D_skip.astype(jnp.float32).reshape(G, R)[..., None] * xc
    return y.reshape(bsz, seqlen, SSD_WIDTH), h_final.reshape(bsz, N_SSD_HEADS, P, N)


def ssd_sample(xs, Bm, Cm, dt, dA, D_skip, ssm_state):
    bsz, tq = xs.shape[0], xs.shape[1]
    G, R, P, N = N_SSD_GROUPS, HEADS_PER_GROUP, SSD_HEAD_DIM, D_STATE
    Dg = D_skip.astype(jnp.float32).reshape(G, R)[..., None]

    def step(h, inp):
        x_t, B_t, C_t, dt_t, dA_t = inp
        h = jnp.exp(dA_t)[..., None, None] * h + jnp.einsum('bgr,bgn,bgrp->bgrpn', dt_t, B_t, x_t)
        y = jnp.einsum('bgn,bgrpn->bgrp', C_t, h) + Dg * x_t
        return h, y

    h0 = ssm_state.astype(jnp.float32).reshape(bsz, G, R, P, N)
    seq_in = tuple(jnp.moveaxis(a, 1, 0) for a in (xs, Bm, Cm, dt, dA))
    h_new, ys = lax.scan(step, h0, seq_in)
    return jnp.moveaxis(ys, 0, 1).reshape(bsz, tq, SSD_WIDTH), h_new.reshape(bsz, N_SSD_HEADS, P, N)


def merge_groups(o_att, y_ssd, z, g_att, g_ssd, dtype):
    att = rmsnorm(o_att, g_att)
    ssd = rmsnorm(y_ssd * jax.nn.silu(z), g_ssd)
    return jnp.concatenate([att, ssd], axis=-1).astype(dtype)


def prompt_mixer(h, w_in, b_f, conv_w, conv_b, dt_bias, A_log, D_skip, g_att, g_ssd):
    q, k, v, logf, z, xbc, dt_raw = project_in(h, w_in, b_f)
    o_att = fox_prompt(q, k, v, logf)
    xpad = jnp.pad(xbc, ((0, 0), (CONV_WIDTH - 1, 0), (0, 0)))
    conv_state = xpad[:, -(CONV_WIDTH - 1):]
    xs, Bm, Cm, dt, dA = ssd_inputs(causal_conv(xpad, conv_w, conv_b), dt_raw, dt_bias, A_log)
    y_ssd, h_final = ssd_prompt(xs, Bm, Cm, dt, dA, D_skip)
    return merge_groups(o_att, y_ssd, z, g_att, g_ssd, h.dtype), (k, v, logf, conv_state, h_final)


def sample_mixer(h, k_past, v_past, logf_past, conv_buf, ssm_state, w_in, b_f, conv_w, conv_b,
                 dt_bias, A_log, D_skip, g_att, g_ssd):
    q, k, v, logf, z, xbc, dt_raw = project_in(h, w_in, b_f)
    o_att = fox_sample(q, k, v, logf, k_past, v_past, logf_past)
    xpad = jnp.concatenate([conv_buf.astype(xbc.dtype), xbc], axis=1)
    conv_state = xpad[:, -(CONV_WIDTH - 1):]
    xs, Bm, Cm, dt, dA = ssd_inputs(causal_conv(xpad, conv_w, conv_b), dt_raw, dt_bias, A_log)
    y_ssd, h_new = ssd_sample(xs, Bm, Cm, dt, dA, D_skip, ssm_state)
    return merge_groups(o_att, y_ssd, z, g_att, g_ssd, h.dtype), (k, v, logf, conv_state, h_new)


def swiglu(h, w_gate, w_up, w_down):
    return (jax.nn.silu(h @ w_gate) * (h @ w_up)) @ w_down


def layer(x, c, mixer, w_ada, b_ada, g_mix, w_out, g_ffn, w_gate, w_up, w_down):
    sh1, sc1, ga1, sh2, sc2, ga2 = ada_terms(c, w_ada, b_ada)
    mix, new_state = mixer(modulate(x, g_mix, sh1, sc1))
    x = x + ga1[:, None] * (mix @ w_out)
    x = x + ga2[:, None] * swiglu(modulate(x, g_ffn, sh2, sc2), w_gate, w_up, w_down)
    return x, new_state


def setup_inputs(seed: int = 0) -> dict:
    key = jax.random.key(seed)
    ks = jax.random.split(key, 32)
    f32 = jnp.float32
    n_pages = PAST_LEN // PAGE_SIZE
    n_pool = (DEC_BATCH * n_pages * 5) // 4

    def nrm(k, shape, scale):
        return scale * jax.random.normal(k, shape, f32)

    page_table = jax.random.permutation(ks[9], n_pool)[:DEC_BATCH * n_pages]
    page_table = page_table.reshape(DEC_BATCH, n_pages).astype(jnp.int32)
    dt0 = jnp.exp(jax.random.uniform(ks[17], (DEPTH, N_SSD_HEADS), f32, math.log(1e-3), math.log(1e-1)))
    return {
        'x_prompt': nrm(ks[0], (BATCH, SEQ, D_MODEL), 1.0),
        'x_sample': nrm(ks[1], (DEC_BATCH, DEC_SEQ, D_MODEL), 1.0),
        'c_prompt': nrm(ks[2], (BATCH, D_MODEL), 1.0),
        'c_sample': nrm(ks[3], (DEC_BATCH, D_MODEL), 1.0),
        'cache_k': nrm(ks[4], (DEPTH, n_pool, PAGE_SIZE, N_ATT_HEADS, HEAD_DIM), 1.0),
        'cache_v': nrm(ks[5], (DEPTH, n_pool, PAGE_SIZE, N_ATT_HEADS, HEAD_DIM), 1.0),
        'cache_logf': jax.nn.log_sigmoid(3.0 + nrm(ks[6], (DEPTH, n_pool, PAGE_SIZE, N_ATT_HEADS), 0.5)),
        'state_conv': nrm(ks[7], (DEPTH, DEC_BATCH, CONV_WIDTH - 1, CONV_DIM), 1.0),
        'state_ssm': nrm(ks[8], (DEPTH, DEC_BATCH, N_SSD_HEADS, SSD_HEAD_DIM, D_STATE), 0.1),
        'page_table': page_table,
        'w_ada': nrm(ks[10], (DEPTH, D_MODEL, 6 * D_MODEL), 0.5 * D_MODEL ** -0.5),
        'b_ada': nrm(ks[11], (DEPTH, 6 * D_MODEL), 0.02),
        'g_mix': 1.0 + nrm(ks[12], (DEPTH, D_MODEL), 0.1),
        'w_in': nrm(ks[13], (DEPTH, D_MODEL, IN_WIDTH), D_MODEL ** -0.5),
        'b_f': 2.0 + nrm(ks[14], (DEPTH, N_ATT_HEADS), 0.5),
        'conv_w': nrm(ks[15], (DEPTH, CONV_WIDTH, CONV_DIM), 0.5),
        'conv_b': nrm(ks[16], (DEPTH, CONV_DIM), 0.02),
        'dt_bias': dt0 + jnp.log(-jnp.expm1(-dt0)),
        'A_log': jnp.log(jax.random.uniform(ks[18], (DEPTH, N_SSD_HEADS), f32, 1.0, 16.0)),
        'D_skip': 1.0 + nrm(ks[19], (DEPTH, N_SSD_HEADS), 0.1),
        'g_att_out': 1.0 + nrm(ks[20], (DEPTH, ATT_WIDTH), 0.1),
        'g_ssd_out': 1.0 + nrm(ks[21], (DEPTH, SSD_WIDTH), 0.1),
        'w_out': nrm(ks[22], (DEPTH, MIX_WIDTH, D_MODEL), MIX_WIDTH ** -0.5),
        'g_ffn': 1.0 + nrm(ks[23], (DEPTH, D_MODEL), 0.1),
        'w_gate': nrm(ks[24], (DEPTH, D_MODEL, FFN_HIDDEN), D_MODEL ** -0.5),
        'w_up': nrm(ks[25], (DEPTH, D_MODEL, FFN_HIDDEN), D_MODEL ** -0.5),
        'w_down': nrm(ks[26], (DEPTH, FFN_HIDDEN, D_MODEL), FFN_HIDDEN ** -0.5),
        'g_final': 1.0 + nrm(ks[27], (D_MODEL,), 0.1),
    }


def reference(x_prompt, x_sample, c_prompt, c_sample, cache_k, cache_v, cache_logf, state_conv, state_ssm,
              page_table, w_ada, b_ada, g_mix, w_in, b_f, conv_w, conv_b, dt_bias, A_log, D_skip,
              g_att_out, g_ssd_out, w_out, g_ffn, w_gate, w_up, w_down, g_final):
    dec_b = page_table.shape[0]
    xp, xs_ = x_prompt, x_sample
    prompt_states, sample_states = [], []
    for l in range(DEPTH):
        xp, st_p = layer(
            xp, c_prompt,
            lambda h, l=l: prompt_mixer(h, w_in[l], b_f[l], conv_w[l], conv_b[l], dt_bias[l], A_log[l],
                                        D_skip[l], g_att_out[l], g_ssd_out[l]),
            w_ada[l], b_ada[l], g_mix[l], w_out[l], g_ffn[l], w_gate[l], w_up[l], w_down[l])
        k_past = cache_k[l][page_table].reshape(dec_b, -1, N_ATT_HEADS, HEAD_DIM)
        v_past = cache_v[l][page_table].reshape(dec_b, -1, N_ATT_HEADS, HEAD_DIM)
        logf_past = cache_logf[l][page_table].reshape(dec_b, -1, N_ATT_HEADS)
        xs_, st_s = layer(
            xs_, c_sample,
            lambda h, l=l, kp=k_past, vp=v_past, fp=logf_past: sample_mixer(
                h, kp, vp, fp, state_conv[l], state_ssm[l], w_in[l], b_f[l], conv_w[l], conv_b[l],
                dt_bias[l], A_log[l], D_skip[l], g_att_out[l], g_ssd_out[l]),
            w_ada[l], b_ada[l], g_mix[l], w_out[l], g_ffn[l], w_gate[l], w_up[l], w_down[l])
        prompt_states.append(st_p)
        sample_states.append(st_s)
    ref_dtypes = (cache_k.dtype, cache_v.dtype, cache_logf.dtype, state_conv.dtype, state_ssm.dtype)
    k_prompt, v_prompt, logf_prompt, conv_prompt, ssm_prompt = [
        jnp.stack(s, axis=0).astype(dt_) for s, dt_ in zip(zip(*prompt_states), ref_dtypes)]
    k_sample, v_sample, logf_sample, conv_sample, ssm_sample = [
        jnp.stack(s, axis=0).astype(dt_) for s, dt_ in zip(zip(*sample_states), ref_dtypes)]
    y_prompt = rmsnorm(xp, g_final).astype(x_prompt.dtype)
    y_sample = rmsnorm(xs_, g_final).astype(x_sample.dtype)
    return (y_prompt, y_sample, k_prompt, v_prompt, logf_prompt, conv_prompt, ssm_prompt,
            k_sample, v_sample, logf_sample, conv_sample, ssm_sample)
```

```python
import functools

import jax
import jax.numpy as jnp
from jax import lax
from jax.experimental import pallas as pl
from jax.experimental.pallas import tpu as pltpu

F32 = jnp.float32
BF16 = jnp.bfloat16

HEAD_DIM = 64
N_HEADS = 8
HEADS_PER_GROUP = 4
D_STATE = 128
CHUNK = 128
NORM_EPS = 1e-6
ATT_SCALE = HEAD_DIM ** -0.5
LANES = 128
SMALL_DT0 = 8
NEG_BIG = -1e30
VMEM_LIMIT = 56 * 1024 * 1024

NT_DIMS = (((1,), (1,)), ((), ()))


def _cparams(sem):
    return pltpu.CompilerParams(dimension_semantics=sem, vmem_limit_bytes=VMEM_LIMIT)


def _silu(x):
    return x * jax.nn.sigmoid(x)


def _softplus(x):
    return jnp.maximum(x, 0.0) + jnp.log1p(jnp.exp(-jnp.abs(x)))


def _rms(x, g):
    return x * lax.rsqrt(jnp.mean(x * x, axis=-1, keepdims=True) + NORM_EPS) * g


def _split3(x):
    hi = x.astype(BF16)
    r1 = x - hi.astype(F32)
    mid = r1.astype(BF16)
    lo = (r1 - mid.astype(F32)).astype(BF16)
    return hi, mid, lo


def _dot(a, b):
    return jnp.dot(a, b, preferred_element_type=F32)


def _dot3_right(x, m):
    hi, mid, lo = _split3(x)
    return _dot(hi, m) + _dot(mid, m) + _dot(lo, m)


def _dot3_left(m, x):
    hi, mid, lo = _split3(x)
    return _dot(m, hi) + _dot(m, mid) + _dot(m, lo)


def _tri(shape, cmp):
    r = lax.broadcasted_iota(jnp.int32, shape, 0)
    c = lax.broadcasted_iota(jnp.int32, shape, 1)
    return cmp(r, c)


def _ada_kernel(c_ref, w_ref, b_ref, o_ref):
    s = _silu(c_ref[...]).astype(BF16)
    o_ref[...] = _dot(s, w_ref[...].astype(BF16)) + b_ref[...]


def _ada(c_all, w_ada, b_ada):
    rows, d = c_all.shape
    n = w_ada.shape[1]
    tn = 512
    return pl.pallas_call(
        _ada_kernel,
        grid=(n // tn,),
        in_specs=[pl.BlockSpec((rows, d), lambda j: (0, 0)),
                  pl.BlockSpec((d, tn), lambda j: (0, j)),
                  pl.BlockSpec((1, tn), lambda j: (0, j))],
        out_specs=pl.BlockSpec((rows, tn), lambda j: (0, j)),
        out_shape=jax.ShapeDtypeStruct((rows, n), F32),
        compiler_params=_cparams(("arbitrary",)),
        name="ada_terms",
    )(c_all, w_ada, b_ada)


def _inproj_kernel(x_ref, sh_ref, sc_ref, g_ref, wb_ref, ws_ref, bias_ref,
                   q_ref, k_ref, v_ref, kb_ref, vb_ref, z_ref, xbc_ref, small_ref, *, att_w):
    h = (_rms(x_ref[...], g_ref[...]) * (1.0 + sc_ref[...]) + sh_ref[...]).astype(BF16)
    a = att_w
    q_ref[...] = (_dot(h, wb_ref[:, 0:a]) * ATT_SCALE).astype(BF16)
    k = _dot(h, wb_ref[:, a:2 * a])
    k_ref[...] = k
    kb_ref[...] = k.astype(BF16)
    v = _dot(h, wb_ref[:, 2 * a:3 * a])
    v_ref[...] = v
    vb_ref[...] = v.astype(BF16)
    z_ref[...] = _dot(h, wb_ref[:, 3 * a:4 * a])
    xbc_ref[...] = _dot(h, wb_ref[:, 4 * a:])
    s = _dot(h, ws_ref[...]) + bias_ref[...]
    lane = lax.broadcasted_iota(jnp.int32, s.shape, 1)
    small_ref[...] = jnp.where(lane < SMALL_DT0, -_softplus(-s), _softplus(s))


def _inproj(x, sh, sc, g_mix, w_big, w_small, bias_small, *, tm, rows_per_mod):
    rows, d = x.shape
    att_w = N_HEADS * HEAD_DIM
    conv_dim = w_big.shape[1] - 4 * att_w
    if rows_per_mod == 1:
        mod_spec = pl.BlockSpec((tm, d), lambda i: (i, 0))
    else:
        blocks_per_seq = rows_per_mod // tm
        mod_spec = pl.BlockSpec((None, 1, d), lambda i: (i // blocks_per_seq, 0, 0))
    row = lambda w: pl.BlockSpec((tm, w), lambda i: (i, 0))
    full = lambda a: pl.BlockSpec(a.shape, lambda i: (0,) * a.ndim)
    sds = lambda w, dt: jax.ShapeDtypeStruct((rows, w), dt)
    return pl.pallas_call(
        functools.partial(_inproj_kernel, att_w=att_w),
        grid=(rows // tm,),
        in_specs=[row(d), mod_spec, mod_spec, full(g_mix), full(w_big), full(w_small), full(bias_small)],
        out_specs=[row(att_w)] * 6 + [row(conv_dim), row(LANES)],
        out_shape=[sds(att_w, BF16), sds(att_w, F32), sds(att_w, F32), sds(att_w, BF16), sds(att_w, BF16),
                   sds(att_w, F32), sds(conv_dim, F32), sds(LANES, F32)],
        compiler_params=_cparams(("arbitrary",)),
        name="in_proj",
    )(x, sh, sc, g_mix, w_big, w_small, bias_small)


def _cumsum_kernel(x_ref, o_ref):
    n_blocks = x_ref.shape[1] // LANES
    upper = _tri((LANES, LANES), lambda r, c: r <= c).astype(BF16)
    carry = jnp.zeros((x_ref.shape[0], 1), F32)
    for c in range(n_blocks):
        cs = _dot3_right(x_ref[:, c * LANES:(c + 1) * LANES], upper) + carry
        o_ref[:, c * LANES:(c + 1) * LANES] = -cs
        carry = cs[:, LANES - 1:LANES]


def _neg_cumsum(x):
    return pl.pallas_call(
        _cumsum_kernel,
        out_shape=jax.ShapeDtypeStruct(x.shape, F32),
        name="logf_cumsum",
    )(x)


def _attn_kernel(q_ref, k_ref, v_ref, nf_ref, o_ref, *, tq):
    qi = pl.program_id(2)
    q = q_ref[...]
    lane_q = lax.broadcasted_iota(jnp.int32, q.shape, 1)
    q_heads = (jnp.where(lane_q < HEAD_DIM, q, jnp.zeros_like(q)),
               jnp.where(lane_q >= HEAD_DIM, q, jnp.zeros_like(q)))

    def step(off, carry, masked):
        kblk = k_ref[pl.ds(off, tq), :]
        vblk = v_ref[pl.ds(off, tq), :]
        out = []
        for hh in range(2):
            m, l, acc = carry[hh]
            s = lax.dot_general(q_heads[hh], kblk, NT_DIMS, preferred_element_type=F32)
            s = s + nf_ref[hh:hh + 1, pl.ds(off, tq)]
            if masked:
                s = jnp.where(_tri(s.shape, lambda r, c: r >= c), s, -jnp.inf)
            m_new = jnp.maximum(m, jnp.max(s, axis=1, keepdims=True))
            alpha = jnp.exp(m - m_new)
            p = jnp.exp(s - m_new)
            l = alpha * l + jnp.sum(p, axis=1, keepdims=True)
            acc = alpha * acc + _dot(p.astype(BF16), vblk)
            out.append((m_new, l, acc))
        return tuple(out)

    init = tuple((jnp.full((tq, 1), NEG_BIG, F32), jnp.zeros((tq, 1), F32), jnp.zeros((tq, LANES), F32))
                 for _ in range(2))
    carry = lax.fori_loop(0, qi, lambda ki, c: step(pl.multiple_of(ki * tq, tq), c, False), init)
    (_, l_a, acc_a), (_, l_b, acc_b) = step(pl.multiple_of(qi * tq, tq), carry, True)
    lane_o = lax.broadcasted_iota(jnp.int32, acc_a.shape, 1)
    o_ref[...] = jnp.where(lane_o < HEAD_DIM, acc_a / l_a, acc_b / l_b)


def _prompt_attention(qb, kb, vb, neg_f, *, batch, seq):
    tq = 512
    nq = seq // tq
    n_pairs = N_HEADS // 2
    return pl.pallas_call(
        functools.partial(_attn_kernel, tq=tq),
        grid=(batch, n_pairs, nq),
        in_specs=[pl.BlockSpec((tq, LANES), lambda b, p, i: (b * nq + i, p)),
                  pl.BlockSpec((seq, LANES), lambda b, p, i: (b, p)),
                  pl.BlockSpec((seq, LANES), lambda b, p, i: (b, p)),
                  pl.BlockSpec((None, 2, seq), lambda b, p, i: (b * n_pairs + p, 0, 0))],
        out_specs=pl.BlockSpec((tq, LANES), lambda b, p, i: (b * nq + i, p)),
        out_shape=jax.ShapeDtypeStruct(qb.shape, F32),
        compiler_params=_cparams(("arbitrary", "arbitrary", "arbitrary")),
        name="fox_prompt_attention",
    )(qb, kb, vb, neg_f)


def _ssd_prompt_kernel(xbc_ref, sa_ref, cw_ref, cb_ref, alog_row_ref, alog_col_ref, dexp_ref,
                       y_ref, ht_ref, ext_ref, hts_ref, *, ssd_w):
    c = pl.program_id(1)
    q_len = CHUNK
    tail = 8

    @pl.when(c == 0)
    def _():
        ext_ref[0:tail, :] = jnp.zeros((tail, ext_ref.shape[1]), F32)
        hts_ref[...] = jnp.zeros(hts_ref.shape, F32)

    x = xbc_ref[...]
    ext_ref[tail:tail + q_len, :] = x
    w = cw_ref[...]
    acc = cb_ref[...] + ext_ref[tail - 3:tail - 3 + q_len, :] * w[0:1]
    acc = acc + ext_ref[tail - 2:tail - 2 + q_len, :] * w[1:2]
    acc = acc + ext_ref[tail - 1:tail - 1 + q_len, :] * w[2:3]
    acc = acc + x * w[3:4]
    u = _silu(acc)
    ext_ref[0:tail, :] = x[q_len - tail:q_len, :]

    sa = sa_ref[...]
    lane = lax.broadcasted_iota(jnp.int32, (1, LANES), 1)
    dt_lanes = (lane >= SMALL_DT0) & (lane < SMALL_DT0 + N_HEADS)
    a_row = jnp.where(dt_lanes, -jnp.exp(alog_row_ref[...]), 0.0)
    lower = _tri((q_len, q_len), lambda r, c_: r >= c_)
    cs_full = _dot3_left(lower.astype(BF16), sa * a_row)
    dt_t = sa.T[SMALL_DT0:SMALL_DT0 + N_HEADS, :]
    da_t = dt_t * (-jnp.exp(alog_col_ref[...]))
    cs_t = _dot3_right(da_t, _tri((q_len, q_len), lambda r, c_: r <= c_).astype(BF16))
    tot_full = jnp.broadcast_to(cs_full[q_len - 1:q_len, :], cs_full.shape)

    lane_p = lax.broadcasted_iota(jnp.int32, (q_len, LANES), 1)
    first = lane_p < HEAD_DIM
    n_groups = N_HEADS // HEADS_PER_GROUP
    gn = n_groups * D_STATE
    for g in range(n_groups):
        bg = u[:, ssd_w + g * D_STATE: ssd_w + (g + 1) * D_STATE]
        cg = u[:, ssd_w + gn + g * D_STATE: ssd_w + gn + (g + 1) * D_STATE].astype(BF16)
        cb = lax.dot_general(cg, bg.astype(BF16), NT_DIMS, preferred_element_type=F32)
        bg_t = bg.T.astype(BF16)
        for k in range(HEADS_PER_GROUP // 2):
            pr = g * (HEADS_PER_GROUP // 2) + k
            x_pair = u[:, pr * LANES:(pr + 1) * LANES]
            x_pair_b = x_pair.astype(BF16)
            ys, cols, tots, dts = [], [], [], []
            for h in (2 * pr, 2 * pr + 1):
                col = cs_full[:, SMALL_DT0 + h:SMALL_DT0 + h + 1]
                seg = col - cs_t[h:h + 1, :]
                lmat = jnp.exp(jnp.where(lower, seg, -jnp.inf))
                mh = (cb * lmat * dt_t[h:h + 1, :]).astype(BF16)
                ys.append(_dot(mh, x_pair_b))
                cols.append(col)
                tots.append(tot_full[:, SMALL_DT0 + h:SMALL_DT0 + h + 1])
                dts.append(sa[:, SMALL_DT0 + h:SMALL_DT0 + h + 1])
            y_diag = jnp.where(first, ys[0], ys[1])
            h_prev = hts_ref[pr]
            e_col = jnp.where(first, jnp.exp(cols[0]), jnp.exp(cols[1]))
            y_off = _dot(cg, h_prev.astype(BF16)) * e_col
            y_ref[:, pr * LANES:(pr + 1) * LANES] = (
                y_diag + y_off + dexp_ref[:, pr * LANES:(pr + 1) * LANES] * x_pair)
            w_pair = jnp.where(first, jnp.exp(tots[0] - cols[0]) * dts[0], jnp.exp(tots[1] - cols[1]) * dts[1])
            new_t = _dot(bg_t, (x_pair * w_pair).astype(BF16))
            dec = jnp.where(first, jnp.exp(tots[0]), jnp.exp(tots[1]))
            hts_ref[pr] = h_prev * dec + new_t

    @pl.when(c == pl.num_programs(1) - 1)
    def _():
        ht_ref[...] = hts_ref[...]


def _ssd_prompt(xbc, small_act, conv_w, conv_b, alog_row, alog_col, d_exp, *, batch, seq):
    ssd_w = d_exp.shape[1]
    conv_dim = xbc.shape[1]
    nc = seq // CHUNK
    n_pairs = N_HEADS // 2
    full = lambda a: pl.BlockSpec(a.shape, lambda b, c: (0,) * a.ndim)
    return pl.pallas_call(
        functools.partial(_ssd_prompt_kernel, ssd_w=ssd_w),
        grid=(batch, nc),
        in_specs=[pl.BlockSpec((CHUNK, conv_dim), lambda b, c: (b * nc + c, 0)),
                  pl.BlockSpec((CHUNK, LANES), lambda b, c: (b * nc + c, 0)),
                  full(conv_w), full(conv_b), full(alog_row), full(alog_col), full(d_exp)],
        out_specs=[pl.BlockSpec((CHUNK, ssd_w), lambda b, c: (b * nc + c, 0)),
                   pl.BlockSpec((None, n_pairs, D_STATE, LANES), lambda b, c: (b, 0, 0, 0))],
        out_shape=[jax.ShapeDtypeStruct((batch * seq, ssd_w), F32),
                   jax.ShapeDtypeStruct((batch, n_pairs, D_STATE, LANES), F32)],
        scratch_shapes=[pltpu.VMEM((CHUNK + 8, conv_dim), F32),
                        pltpu.VMEM((n_pairs, D_STATE, LANES), F32)],
        compiler_params=_cparams(("arbitrary", "arbitrary")),
        name="ssd_prompt",
    )(xbc, small_act, conv_w, conv_b, alog_row, alog_col, d_exp)


def _merge_kernel(o_ref, y_ref, z_ref, x_ref, ga_ref, gatt_ref, gssd_ref, w_ref, out_ref, *, att_w):
    att = _rms(o_ref[...], gatt_ref[...]).astype(BF16)
    ssd = _rms(y_ref[...] * _silu(z_ref[...]), gssd_ref[...]).astype(BF16)
    mix = _dot(att, w_ref[0:att_w, :]) + _dot(ssd, w_ref[att_w:, :])
    out_ref[...] = x_ref[...] + ga_ref[...] * mix


def _mod_spec(tm, d, rows_per_mod):
    if rows_per_mod == 1:
        return pl.BlockSpec((tm, d), lambda i, *_: (i, 0))
    blocks_per_seq = rows_per_mod // tm
    return pl.BlockSpec((None, 1, d), lambda i, *_: (i // blocks_per_seq, 0, 0))


def _merge_outproj(o_att, y_ssd, z, x, ga1, g_att, g_ssd, w_out, *, tm, rows_per_mod):
    rows, d = x.shape
    att_w = o_att.shape[1]
    row = lambda w: pl.BlockSpec((tm, w), lambda i: (i, 0))
    full = lambda a: pl.BlockSpec(a.shape, lambda i: (0,) * a.ndim)
    return pl.pallas_call(
        functools.partial(_merge_kernel, att_w=att_w),
        grid=(rows // tm,),
        in_specs=[row(att_w), row(att_w), row(att_w), row(d), _mod_spec(tm, d, rows_per_mod),
                  full(g_att), full(g_ssd), full(w_out)],
        out_specs=row(d),
        out_shape=jax.ShapeDtypeStruct((rows, d), F32),
        compiler_params=_cparams(("arbitrary",)),
        name="merge_out_proj",
    )(o_att, y_ssd, z, x, ga1, g_att, g_ssd, w_out)


def _ffn_kernel(x_ref, sh_ref, sc_ref, ga_ref, g_ref, wg_ref, wu_ref, wd_ref, gf_ref, out_ref, h_ref, acc_ref):
    j = pl.program_id(1)

    @pl.when(j == 0)
    def _():
        h_ref[...] = (_rms(x_ref[...], g_ref[...]) * (1.0 + sc_ref[...]) + sh_ref[...]).astype(BF16)
        acc_ref[...] = jnp.zeros(acc_ref.shape, F32)

    h = h_ref[...]
    a = (_silu(_dot(h, wg_ref[...])) * _dot(h, wu_ref[...])).astype(BF16)
    acc_ref[...] += _dot(a, wd_ref[...])

    @pl.when(j == pl.num_programs(1) - 1)
    def _():
        out_ref[...] = _rms(x_ref[...] + ga_ref[...] * acc_ref[...], gf_ref[...])


def _ffn_final(x, sh, sc, ga, g_ffn, w_gate, w_up, w_down, g_final, *, tm, rows_per_mod):
    rows, d = x.shape
    hidden = w_gate.shape[1]
    th = hidden // 2
    mod = _mod_spec(tm, d, rows_per_mod)
    full = lambda a: pl.BlockSpec(a.shape, lambda i, j: (0,) * a.ndim)
    return pl.pallas_call(
        _ffn_kernel,
        grid=(rows // tm, hidden // th),
        in_specs=[pl.BlockSpec((tm, d), lambda i, j: (i, 0)), mod, mod, mod, full(g_ffn),
                  pl.BlockSpec((d, th), lambda i, j: (0, j)),
                  pl.BlockSpec((d, th), lambda i, j: (0, j)),
                  pl.BlockSpec((th, d), lambda i, j: (j, 0)),
                  full(g_final)],
        out_specs=pl.BlockSpec((tm, d), lambda i, j: (i, 0)),
        out_shape=jax.ShapeDtypeStruct((rows, d), F32),
        scratch_shapes=[pltpu.VMEM((tm, d), BF16), pltpu.VMEM((tm, d), F32)],
        compiler_params=_cparams(("arbitrary", "arbitrary")),
        name="ffn_final_norm",
    )(x, sh, sc, ga, g_ffn, w_gate, w_up, w_down, g_final)


def _page_suffix_kernel(x_ref, st_ref, tot_ref):
    x = x_ref[...]
    strict = _tri((LANES, LANES), lambda r, c: r > c).astype(BF16)
    st_ref[...] = _dot3_right(x, strict)
    tot_ref[...] = _dot3_right(x, jnp.ones((LANES, LANES), BF16))


def _page_suffix(lf_t):
    rows = lf_t.shape[0]
    tr = 2048
    spec = pl.BlockSpec((tr, LANES), lambda i: (i, 0))
    return pl.pallas_call(
        _page_suffix_kernel,
        grid=(rows // tr,),
        in_specs=[spec],
        out_specs=[spec, spec],
        out_shape=[jax.ShapeDtypeStruct(lf_t.shape, F32)] * 2,
        compiler_params=_cparams(("arbitrary",)),
        name="page_logf_suffix",
    )(lf_t)


def _decode_kernel(pt_ref, q_ref, kn_ref, vn_ref, cn_ref, k_ref, v_ref, st_ref, tot_ref, o_ref,
                   m_ref, l_ref, acc_ref, carry_ref):
    j = pl.program_id(1)

    @pl.when(j == 0)
    def _():
        m_ref[...] = jnp.full(m_ref.shape, NEG_BIG, F32)
        l_ref[...] = jnp.zeros(l_ref.shape, F32)
        acc_ref[...] = jnp.zeros(acc_ref.shape, F32)
        carry_ref[...] = cn_ref[...]

    q8 = q_ref[...]
    row_q = lax.broadcasted_iota(jnp.int32, q8.shape, 0)
    s = st_ref[...] + carry_ref[...]
    carry_ref[...] = carry_ref[...] + tot_ref[...]
    for h in range(N_HEADS):
        kh = k_ref[pl.ds(h, CHUNK, stride=N_HEADS), :].astype(BF16)
        qm = jnp.where(row_q == h, q8, jnp.zeros_like(q8))
        s = s + lax.dot_general(qm, kh, NT_DIMS, preferred_element_type=F32)
    m_old = m_ref[...]
    m_new = jnp.maximum(m_old, jnp.max(s, axis=1, keepdims=True))
    alpha = jnp.exp(m_old - m_new)
    p = jnp.exp(s - m_new)
    l_ref[...] = alpha * l_ref[...] + jnp.sum(p, axis=1, keepdims=True)
    m_ref[...] = m_new
    pb = p.astype(BF16)
    row_p = lax.broadcasted_iota(jnp.int32, pb.shape, 0)
    pv = jnp.zeros(acc_ref.shape, F32)
    for h in range(N_HEADS):
        vh = v_ref[pl.ds(h, CHUNK, stride=N_HEADS), :].astype(BF16)
        pm = jnp.where(row_p == h, pb, jnp.zeros_like(pb))
        pv = pv + _dot(pm, vh)
    acc_ref[...] = alpha * acc_ref[...] + pv

    @pl.when(j == pl.num_programs(1) - 1)
    def _():
        s_new = jnp.sum(q8.astype(F32) * kn_ref[...].astype(F32), axis=1, keepdims=True)
        m_fin = jnp.maximum(m_ref[...], s_new)
        a_fin = jnp.exp(m_ref[...] - m_fin)
        p_new = jnp.exp(s_new - m_fin)
        num = a_fin * acc_ref[...] + p_new.astype(BF16).astype(F32) * vn_ref[...].astype(F32)
        o_ref[...] = num / (a_fin * l_ref[...] + p_new)


def _decode_attention(page_table, q8, kn8, vn8, cn_bc, cache_k, cache_v, st, tot):
    dec_b, n_pages = page_table.shape
    rows = cache_k.shape[1]
    per_b = lambda shape: pl.BlockSpec((None,) + shape, lambda b, j, pt: (b, 0, 0))
    paged = lambda shape: pl.BlockSpec((None,) + shape, lambda b, j, pt: (pt[b, n_pages - 1 - j], 0, 0))
    return pl.pallas_call(
        _decode_kernel,
        grid_spec=pltpu.PrefetchScalarGridSpec(
            num_scalar_prefetch=1,
            grid=(dec_b, n_pages),
            in_specs=[per_b((N_HEADS, HEAD_DIM)), per_b((N_HEADS, HEAD_DIM)), per_b((N_HEADS, HEAD_DIM)),
                      per_b((N_HEADS, CHUNK)),
                      paged((rows, HEAD_DIM)), paged((rows, HEAD_DIM)),
                      paged((N_HEADS, CHUNK)), paged((N_HEADS, CHUNK))],
            out_specs=per_b((N_HEADS, HEAD_DIM)),
            scratch_shapes=[pltpu.VMEM((N_HEADS, 1), F32), pltpu.VMEM((N_HEADS, 1), F32),
                            pltpu.VMEM((N_HEADS, HEAD_DIM), F32), pltpu.VMEM((N_HEADS, CHUNK), F32)]),
        out_shape=jax.ShapeDtypeStruct((dec_b, N_HEADS, HEAD_DIM), F32),
        compiler_params=_cparams(("arbitrary", "arbitrary")),
        name="fox_decode_attention",
    )(page_table, q8, kn8, vn8, cn_bc, cache_k, cache_v, st, tot)


def _sample_conv_kernel(s0_ref, s1_ref, s2_ref, x_ref, sa_ref, cw_ref, cb_ref, alog_row_ref, u_ref, dec_ref):
    w = cw_ref[...]
    acc = cb_ref[...] + s0_ref[...] * w[0:1]
    acc = acc + s1_ref[...] * w[1:2]
    acc = acc + s2_ref[...] * w[2:3]
    acc = acc + x_ref[...] * w[3:4]
    u_ref[...] = _silu(acc)
    dec_ref[...] = jnp.exp(sa_ref[...] * (-jnp.exp(alog_row_ref[...])))


def _sample_conv(s0, s1, s2, xbc, small_act, conv_w, conv_b, alog_row):
    return pl.pallas_call(
        _sample_conv_kernel,
        out_shape=[jax.ShapeDtypeStruct(xbc.shape, F32), jax.ShapeDtypeStruct(small_act.shape, F32)],
        name="sample_conv",
    )(s0, s1, s2, xbc, small_act, conv_w, conv_b, alog_row)


def _ssd_step_kernel(dt_ref, dec_ref, dsk_ref, h_ref, xcol_ref, bc_ref, hn_ref, ycol_ref):
    b = pl.program_id(0)
    gn = (N_HEADS // HEADS_PER_GROUP) * D_STATE
    for h in range(N_HEADS):
        g = h // HEADS_PER_GROUP
        xc = xcol_ref[h * HEAD_DIM:(h + 1) * HEAD_DIM, :]
        b_row = bc_ref[:, g * D_STATE:(g + 1) * D_STATE]
        c_row = bc_ref[:, gn + g * D_STATE:gn + (g + 1) * D_STATE]
        hn = dec_ref[b, h] * h_ref[h] + (dt_ref[b, h] * xc) * b_row
        hn_ref[h] = hn
        ycol_ref[h * HEAD_DIM:(h + 1) * HEAD_DIM, :] = (
            jnp.sum(hn * c_row, axis=1, keepdims=True) + dsk_ref[0, h] * xc)


def _ssd_step(dt, dec, d_skip, state, xcol, bc):
    dec_b = state.shape[0]
    smem = pl.BlockSpec(memory_space=pltpu.SMEM)
    st_spec = pl.BlockSpec((None,) + state.shape[1:], lambda b: (b, 0, 0, 0))
    col_spec = pl.BlockSpec((None,) + xcol.shape[1:], lambda b: (b, 0, 0))
    return pl.pallas_call(
        _ssd_step_kernel,
        grid=(dec_b,),
        in_specs=[smem, smem, smem, st_spec, col_spec,
                  pl.BlockSpec((None,) + bc.shape[1:], lambda b: (b, 0, 0))],
        out_specs=[st_spec, col_spec],
        out_shape=[jax.ShapeDtypeStruct(state.shape, F32), jax.ShapeDtypeStruct(xcol.shape, F32)],
        compiler_params=_cparams(("arbitrary",)),
        name="ssd_step",
    )(dt, dec, d_skip, state, xcol, bc)


def kernel(x_prompt, x_sample, c_prompt, c_sample, cache_k, cache_v, cache_logf, state_conv, state_ssm, page_table, w_ada, b_ada, g_mix, w_in, b_f, conv_w, conv_b, dt_bias, A_log, D_skip, g_att_out, g_ssd_out, w_out, g_ffn, w_gate, w_up, w_down, g_final):
    batch, seq, d = x_prompt.shape
    dec_b = x_sample.shape[0]
    att_w = N_HEADS * HEAD_DIM
    ssd_w = g_ssd_out.shape[1]
    conv_dim = conv_w.shape[2]
    n_pool = cache_k.shape[1]
    l = 0

    wi = w_in[l]
    o_f = 3 * att_w
    o_z = o_f + N_HEADS
    o_x = o_z + ssd_w
    o_dt = o_x + conv_dim
    w_big = jnp.concatenate([wi[:, :o_f], wi[:, o_z:o_dt]], axis=1).astype(BF16)
    pad = jnp.zeros((d, LANES - 2 * N_HEADS), F32)
    w_small = jnp.concatenate([wi[:, o_f:o_z], wi[:, o_dt:], pad], axis=1).astype(BF16)
    zpad = jnp.zeros((LANES - 2 * N_HEADS,), F32)
    bias_small = jnp.concatenate([b_f[l], dt_bias[l], zpad])[None, :]
    alog_row = jnp.concatenate([jnp.zeros((N_HEADS,), F32), A_log[l], zpad])[None, :]
    alog_col = A_log[l][:, None]
    d_exp = jnp.repeat(D_skip[l], HEAD_DIM)[None, :]
    w_out_b = w_out[l].astype(BF16)
    wg_b, wu_b, wd_b = w_gate[l].astype(BF16), w_up[l].astype(BF16), w_down[l].astype(BF16)
    g_final2 = g_final[None, :]

    ada_rows = 144
    c_all = jnp.concatenate([c_prompt, c_sample, jnp.zeros((ada_rows - batch - dec_b, d), F32)], axis=0)
    mod = _ada(c_all, w_ada[l], b_ada[l][None, :])
    mod_p = mod[:batch].reshape(batch, 6, 1, d)
    sh1_p, sc1_p, ga1_p, sh2_p, sc2_p, ga2_p = (mod_p[:, i] for i in range(6))
    mod_s = mod[batch:batch + dec_b].reshape(dec_b, 6, d)
    sh1_s, sc1_s, ga1_s, sh2_s, sc2_s, ga2_s = (mod_s[:, i] for i in range(6))

    xp = x_prompt.reshape(batch * seq, d)
    tm_p = 512
    qb, k_p, v_p, kb, vb, z_p, xbc_p, sa_p = _inproj(
        xp, sh1_p, sc1_p, g_mix[l][None, :], w_big, w_small, bias_small, tm=tm_p, rows_per_mod=seq)
    logf_p = sa_p[:, :N_HEADS].reshape(batch, seq, N_HEADS)
    neg_f = _neg_cumsum(jnp.swapaxes(logf_p, 1, 2).reshape(batch * N_HEADS, seq))
    o_att_p = _prompt_attention(qb, kb, vb, neg_f.reshape(batch * N_HEADS // 2, 2, seq), batch=batch, seq=seq)
    y_ssd_p, ht_p = _ssd_prompt(xbc_p, sa_p, conv_w[l], conv_b[l][None, :], alog_row, alog_col, d_exp,
                                batch=batch, seq=seq)
    x1_p = _merge_outproj(o_att_p, y_ssd_p, z_p, xp, ga1_p, g_att_out[l][None, :], g_ssd_out[l][None, :],
                          w_out_b, tm=tm_p, rows_per_mod=seq)
    y_p = _ffn_final(x1_p, sh2_p, sc2_p, ga2_p, g_ffn[l][None, :], wg_b, wu_b, wd_b, g_final2,
                     tm=tm_p, rows_per_mod=seq)

    xs = x_sample.reshape(dec_b, d)
    qs, k_s, v_s, ks_b, vs_b, z_s, xbc_s, sa_s = _inproj(
        xs, sh1_s, sc1_s, g_mix[l][None, :], w_big, w_small, bias_small, tm=dec_b, rows_per_mod=1)
    heads = lambda a: a.reshape(dec_b, N_HEADS, HEAD_DIM)
    lf_t = jnp.swapaxes(cache_logf[l], 1, 2).reshape(n_pool * N_HEADS, CHUNK)
    st, tot = _page_suffix(lf_t)
    cn_bc = jnp.broadcast_to(sa_s[:, :N_HEADS, None], (dec_b, N_HEADS, CHUNK))
    o_att_s = _decode_attention(
        page_table, heads(qs), heads(ks_b), heads(vs_b), cn_bc,
        cache_k[l].reshape(n_pool, CHUNK * N_HEADS, HEAD_DIM), cache_v[l].reshape(n_pool, CHUNK * N_HEADS, HEAD_DIM),
        st.reshape(n_pool, N_HEADS, CHUNK), tot.reshape(n_pool, N_HEADS, CHUNK)).reshape(dec_b, att_w)
    sc_l = state_conv[l]
    u_s, dec_s = _sample_conv(sc_l[:, 0], sc_l[:, 1], sc_l[:, 2], xbc_s, sa_s, conv_w[l], conv_b[l][None, :], alog_row)
    dt_s = sa_s[:, SMALL_DT0:SMALL_DT0 + N_HEADS]
    h_new, ycol = _ssd_step(dt_s, dec_s[:, SMALL_DT0:SMALL_DT0 + N_HEADS], D_skip[l][None, :], state_ssm[l],
                            u_s[:, :ssd_w].reshape(dec_b, ssd_w, 1), u_s[:, ssd_w:].reshape(dec_b, 1, conv_dim - ssd_w))
    x1_s = _merge_outproj(o_att_s, ycol.reshape(dec_b, ssd_w), z_s, xs, ga1_s, g_att_out[l][None, :],
                          g_ssd_out[l][None, :], w_out_b, tm=dec_b, rows_per_mod=1)
    y_s = _ffn_final(x1_s, sh2_s, sc2_s, ga2_s, g_ffn[l][None, :], wg_b, wu_b, wd_b, g_final2,
                     tm=dec_b, rows_per_mod=1)

    ht = ht_p.reshape(batch, N_HEADS // 2, D_STATE, 2, HEAD_DIM)
    ssm_prompt = jnp.transpose(ht, (0, 1, 3, 4, 2)).reshape(1, batch, N_HEADS, HEAD_DIM, D_STATE)
    conv_prompt = xbc_p.reshape(batch, seq, conv_dim)[:, seq - 3:, :][None]
    conv_sample = jnp.concatenate([sc_l[:, 1:], xbc_s[:, None, :]], axis=1)[None]
    return (y_p.reshape(batch, seq, d), y_s.reshape(dec_b, 1, d),
            k_p.reshape(1, batch, seq, N_HEADS, HEAD_DIM), v_p.reshape(1, batch, seq, N_HEADS, HEAD_DIM),
            logf_p[None],
            conv_prompt, ssm_prompt,
            k_s.reshape(1, dec_b, 1, N_HEADS, HEAD_DIM), v_s.reshape(1, dec_b, 1, N_HEADS, HEAD_DIM),
            sa_s[:, :N_HEADS].reshape(1, dec_b, 1, N_HEADS),
            conv_sample, h_new[None])
```

```python
import functools

import jax
import jax.numpy as jnp
from jax import lax
from jax.experimental import pallas as pl
from jax.experimental.pallas import tpu as pltpu

F32 = jnp.float32
BF16 = jnp.bfloat16

HEAD_DIM = 64
N_HEADS = 8
HEADS_PER_GROUP = 4
D_STATE = 128
CHUNK = 128
NORM_EPS = 1e-6
ATT_SCALE = HEAD_DIM ** -0.5
LANES = 128
SMALL_DT0 = 8
NEG_BIG = -1e30
VMEM_LIMIT = 56 * 1024 * 1024

NT_DIMS = (((1,), (1,)), ((), ()))


def _cparams(sem):
    return pltpu.CompilerParams(dimension_semantics=sem, vmem_limit_bytes=VMEM_LIMIT)


def _silu(x):
    return x * jax.nn.sigmoid(x)


def _softplus(x):
    return jnp.maximum(x, 0.0) + jnp.log1p(jnp.exp(-jnp.abs(x)))


def _rms(x, g):
    return x * lax.rsqrt(jnp.mean(x * x, axis=-1, keepdims=True) + NORM_EPS) * g


def _split3(x):
    hi = x.astype(BF16)
    r1 = x - hi.astype(F32)
    mid = r1.astype(BF16)
    lo = (r1 - mid.astype(F32)).astype(BF16)
    return hi, mid, lo


def _dot(a, b):
    return jnp.dot(a, b, preferred_element_type=F32)


def _dot3_right(x, m):
    hi, mid, lo = _split3(x)
    return _dot(hi, m) + _dot(mid, m) + _dot(lo, m)


def _dot3_left(m, x):
    hi, mid, lo = _split3(x)
    return _dot(m, hi) + _dot(m, mid) + _dot(m, lo)


def _tri(shape, cmp):
    r = lax.broadcasted_iota(jnp.int32, shape, 0)
    c = lax.broadcasted_iota(jnp.int32, shape, 1)
    return cmp(r, c)


def _ada_kernel(c_ref, w_ref, b_ref, o_ref):
    s = _silu(c_ref[...]).astype(BF16)
    o_ref[...] = _dot(s, w_ref[...].astype(BF16)) + b_ref[...]


def _ada(c_all, w_ada, b_ada):
    rows, d = c_all.shape
    n = w_ada.shape[1]
    tn = 512
    return pl.pallas_call(
        _ada_kernel,
        grid=(n // tn,),
        in_specs=[pl.BlockSpec((rows, d), lambda j: (0, 0)),
                  pl.BlockSpec((d, tn), lambda j: (0, j)),
                  pl.BlockSpec((1, tn), lambda j: (0, j))],
        out_specs=pl.BlockSpec((rows, tn), lambda j: (0, j)),
        out_shape=jax.ShapeDtypeStruct((rows, n), F32),
        compiler_params=_cparams(("arbitrary",)),
        name="ada_terms",
    )(c_all, w_ada, b_ada)


def _inproj_kernel(x_ref, sh_ref, sc_ref, g_ref, wb_ref, ws_ref, bias_ref,
                   q_ref, k_ref, v_ref, kb_ref, vb_ref, z_ref, xbc_ref, small_ref, *, att_w):
    h = (_rms(x_ref[...], g_ref[...]) * (1.0 + sc_ref[...]) + sh_ref[...]).astype(BF16)
    a = att_w
    q_ref[...] = (_dot(h, wb_ref[:, 0:a]) * ATT_SCALE).astype(BF16)
    k = _dot(h, wb_ref[:, a:2 * a])
    k_ref[...] = k
    kb_ref[...] = k.astype(BF16)
    v = _dot(h, wb_ref[:, 2 * a:3 * a])
    v_ref[...] = v
    vb_ref[...] = v.astype(BF16)
    z_ref[...] = _dot(h, wb_ref[:, 3 * a:4 * a])
    xbc_ref[...] = _dot(h, wb_ref[:, 4 * a:])
    s = _dot(h, ws_ref[...]) + bias_ref[...]
    lane = lax.broadcasted_iota(jnp.int32, s.shape, 1)
    small_ref[...] = jnp.where(lane < SMALL_DT0, -_softplus(-s), _softplus(s))


def _inproj(x, sh, sc, g_mix, w_big, w_small, bias_small, *, tm, rows_per_mod):
    rows, d = x.shape
    att_w = N_HEADS * HEAD_DIM
    conv_dim = w_big.shape[1] - 4 * att_w
    if rows_per_mod == 1:
        mod_spec = pl.BlockSpec((tm, d), lambda i: (i, 0))
    else:
        blocks_per_seq = rows_per_mod // tm
        mod_spec = pl.BlockSpec((None, 1, d), lambda i: (i // blocks_per_seq, 0, 0))
    row = lambda w: pl.BlockSpec((tm, w), lambda i: (i, 0))
    full = lambda a: pl.BlockSpec(a.shape, lambda i: (0,) * a.ndim)
    sds = lambda w, dt: jax.ShapeDtypeStruct((rows, w), dt)
    return pl.pallas_call(
        functools.partial(_inproj_kernel, att_w=att_w),
        grid=(rows // tm,),
        in_specs=[row(d), mod_spec, mod_spec, full(g_mix), full(w_big), full(w_small), full(bias_small)],
        out_specs=[row(att_w)] * 6 + [row(conv_dim), row(LANES)],
        out_shape=[sds(att_w, BF16), sds(att_w, F32), sds(att_w, F32), sds(att_w, BF16), sds(att_w, BF16),
                   sds(att_w, F32), sds(conv_dim, F32), sds(LANES, F32)],
        compiler_params=_cparams(("arbitrary",)),
        name="in_proj",
    )(x, sh, sc, g_mix, w_big, w_small, bias_small)


def _cumsum_kernel(x_ref, o_ref):
    n_blocks = x_ref.shape[1] // LANES
    upper = _tri((LANES, LANES), lambda r, c: r <= c).astype(BF16)
    carry = jnp.zeros((x_ref.shape[0], 1), F32)
    for c in range(n_blocks):
        cs = _dot3_right(x_ref[:, c * LANES:(c + 1) * LANES], upper) + carry
        o_ref[:, c * LANES:(c + 1) * LANES] = -cs
        carry = cs[:, LANES - 1:LANES]


def _neg_cumsum(x):
    return pl.pallas_call(
        _cumsum_kernel,
        out_shape=jax.ShapeDtypeStruct(x.shape, F32),
        name="logf_cumsum",
    )(x)


def _attn_kernel(q_ref, k_ref, v_ref, nf_ref, o_ref, *, tq):
    qi = pl.program_id(2)
    q = q_ref[...]
    lane_q = lax.broadcasted_iota(jnp.int32, q.shape, 1)
    q_heads = (jnp.where(lane_q < HEAD_DIM, q, jnp.zeros_like(q)),
               jnp.where(lane_q >= HEAD_DIM, q, jnp.zeros_like(q)))

    def step(off, carry, masked):
        kblk = k_ref[pl.ds(off, tq), :]
        vblk = v_ref[pl.ds(off, tq), :]
        out = []
        for hh in range(2):
            m, l, acc = carry[hh]
            s = lax.dot_general(q_heads[hh], kblk, NT_DIMS, preferred_element_type=F32)
            s = s + nf_ref[hh:hh + 1, pl.ds(off, tq)]
            if masked:
                s = jnp.where(_tri(s.shape, lambda r, c: r >= c), s, -jnp.inf)
            m_new = jnp.maximum(m, jnp.max(s, axis=1, keepdims=True))
            alpha = jnp.exp(m - m_new)
            p = jnp.exp(s - m_new)
            l = alpha * l + jnp.sum(p, axis=1, keepdims=True)
            acc = alpha * acc + _dot(p.astype(BF16), vblk)
            out.append((m_new, l, acc))
        return tuple(out)

    init = tuple((jnp.full((tq, 1), NEG_BIG, F32), jnp.zeros((tq, 1), F32), jnp.zeros((tq, LANES), F32))
                 for _ in range(2))
    carry = lax.fori_loop(0, qi, lambda ki, c: step(pl.multiple_of(ki * tq, tq), c, False), init)
    (_, l_a, acc_a), (_, l_b, acc_b) = step(pl.multiple_of(qi * tq, tq), carry, True)
    lane_o = lax.broadcasted_iota(jnp.int32, acc_a.shape, 1)
    o_ref[...] = jnp.where(lane_o < HEAD_DIM, acc_a / l_a, acc_b / l_b)


def _prompt_attention(qb, kb, vb, neg_f, *, batch, seq):
    tq = 512
    nq = seq // tq
    n_pairs = N_HEADS // 2
    return pl.pallas_call(
        functools.partial(_attn_kernel, tq=tq),
        grid=(batch, n_pairs, nq),
        in_specs=[pl.BlockSpec((tq, LANES), lambda b, p, i: (b * nq + i, p)),
                  pl.BlockSpec((seq, LANES), lambda b, p, i: (b, p)),
                  pl.BlockSpec((seq, LANES), lambda b, p, i: (b, p)),
                  pl.BlockSpec((None, 2, seq), lambda b, p, i: (b * n_pairs + p, 0, 0))],
        out_specs=pl.BlockSpec((tq, LANES), lambda b, p, i: (b * nq + i, p)),
        out_shape=jax.ShapeDtypeStruct(qb.shape, F32),
        compiler_params=_cparams(("arbitrary", "arbitrary", "arbitrary")),
        name="fox_prompt_attention",
    )(qb, kb, vb, neg_f)


def _ssd_prompt_kernel(xbc_ref, sa_ref, cw_ref, cb_ref, alog_row_ref, alog_col_ref, dexp_ref,
                       y_ref, ht_ref, ext_ref, hts_ref, *, ssd_w):
    c = pl.program_id(1)
    q_len = CHUNK
    tail = 8

    @pl.when(c == 0)
    def _():
        ext_ref[0:tail, :] = jnp.zeros((tail, ext_ref.shape[1]), F32)
        hts_ref[...] = jnp.zeros(hts_ref.shape, F32)

    x = xbc_ref[...]
    ext_ref[tail:tail + q_len, :] = x
    w = cw_ref[...]
    acc = cb_ref[...] + ext_ref[tail - 3:tail - 3 + q_len, :] * w[0:1]
    acc = acc + ext_ref[tail - 2:tail - 2 + q_len, :] * w[1:2]
    acc = acc + ext_ref[tail - 1:tail - 1 + q_len, :] * w[2:3]
    acc = acc + x * w[3:4]
    u = _silu(acc)
    ext_ref[0:tail, :] = x[q_len - tail:q_len, :]

    sa = sa_ref[...]
    lane = lax.broadcasted_iota(jnp.int32, (1, LANES), 1)
    dt_lanes = (lane >= SMALL_DT0) & (lane < SMALL_DT0 + N_HEADS)
    a_row = jnp.where(dt_lanes, -jnp.exp(alog_row_ref[...]), 0.0)
    lower = _tri((q_len, q_len), lambda r, c_: r >= c_)
    cs_full = _dot3_left(lower.astype(BF16), sa * a_row)
    dt_t = sa.T[SMALL_DT0:SMALL_DT0 + N_HEADS, :]
    da_t = dt_t * (-jnp.exp(alog_col_ref[...]))
    cs_t = _dot3_right(da_t, _tri((q_len, q_len), lambda r, c_: r <= c_).astype(BF16))
    tot_full = jnp.broadcast_to(cs_full[q_len - 1:q_len, :], cs_full.shape)

    lane_p = lax.broadcasted_iota(jnp.int32, (q_len, LANES), 1)
    first = lane_p < HEAD_DIM
    n_groups = N_HEADS // HEADS_PER_GROUP
    gn = n_groups * D_STATE
    for g in range(n_groups):
        bg = u[:, ssd_w + g * D_STATE: ssd_w + (g + 1) * D_STATE]
        cg = u[:, ssd_w + gn + g * D_STATE: ssd_w + gn + (g + 1) * D_STATE].astype(BF16)
        cb = lax.dot_general(cg, bg.astype(BF16), NT_DIMS, preferred_element_type=F32)
        bg_t = bg.T.astype(BF16)
        for k in range(HEADS_PER_GROUP // 2):
            pr = g * (HEADS_PER_GROUP // 2) + k
            x_pair = u[:, pr * LANES:(pr + 1) * LANES]
            x_pair_b = x_pair.astype(BF16)
            ys, cols, tots, dts = [], [], [], []
            for h in (2 * pr, 2 * pr + 1):
                col = cs_full[:, SMALL_DT0 + h:SMALL_DT0 + h + 1]
                seg = col - cs_t[h:h + 1, :]
                lmat = jnp.exp(jnp.where(lower, seg, -jnp.inf))
                mh = (cb * lmat * dt_t[h:h + 1, :]).astype(BF16)
                ys.append(_dot(mh, x_pair_b))
                cols.append(col)
                tots.append(tot_full[:, SMALL_DT0 + h:SMALL_DT0 + h + 1])
                dts.append(sa[:, SMALL_DT0 + h:SMALL_DT0 + h + 1])
            y_diag = jnp.where(first, ys[0], ys[1])
            h_prev = hts_ref[pr]
            e_col = jnp.where(first, jnp.exp(cols[0]), jnp.exp(cols[1]))
            y_off = _dot(cg, h_prev.astype(BF16)) * e_col
            y_ref[:, pr * LANES:(pr + 1) * LANES] = (
                y_diag + y_off + dexp_ref[:, pr * LANES:(pr + 1) * LANES] * x_pair)
            w_pair = jnp.where(first, jnp.exp(tots[0] - cols[0]) * dts[0], jnp.exp(tots[1] - cols[1]) * dts[1])
            new_t = _dot(bg_t, (x_pair * w_pair).astype(BF16))
            dec = jnp.where(first, jnp.exp(tots[0]), jnp.exp(tots[1]))
            hts_ref[pr] = h_prev * dec + new_t

    @pl.when(c == pl.num_programs(1) - 1)
    def _():
        ht_ref[...] = hts_ref[...]


def _ssd_prompt(xbc, small_act, conv_w, conv_b, alog_row, alog_col, d_exp, *, batch, seq):
    ssd_w = d_exp.shape[1]
    conv_dim = xbc.shape[1]
    nc = seq // CHUNK
    n_pairs = N_HEADS // 2
    full = lambda a: pl.BlockSpec(a.shape, lambda b, c: (0,) * a.ndim)
    return pl.pallas_call(
        functools.partial(_ssd_prompt_kernel, ssd_w=ssd_w),
        grid=(batch, nc),
        in_specs=[pl.BlockSpec((CHUNK, conv_dim), lambda b, c: (b * nc + c, 0)),
                  pl.BlockSpec((CHUNK, LANES), lambda b, c: (b * nc + c, 0)),
                  full(conv_w), full(conv_b), full(alog_row), full(alog_col), full(d_exp)],
        out_specs=[pl.BlockSpec((CHUNK, ssd_w), lambda b, c: (b * nc + c, 0)),
                   pl.BlockSpec((None, n_pairs, D_STATE, LANES), lambda b, c: (b, 0, 0, 0))],
        out_shape=[jax.ShapeDtypeStruct((batch * seq, ssd_w), F32),
                   jax.ShapeDtypeStruct((batch, n_pairs, D_STATE, LANES), F32)],
        scratch_shapes=[pltpu.VMEM((CHUNK + 8, conv_dim), F32),
                        pltpu.VMEM((n_pairs, D_STATE, LANES), F32)],
        compiler_params=_cparams(("arbitrary", "arbitrary")),
        name="ssd_prompt",
    )(xbc, small_act, conv_w, conv_b, alog_row, alog_col, d_exp)


def _merge_kernel(o_ref, y_ref, z_ref, x_ref, ga_ref, gatt_ref, gssd_ref, w_ref, out_ref, *, att_w):
    att = _rms(o_ref[...], gatt_ref[...]).astype(BF16)
    ssd = _rms(y_ref[...] * _silu(z_ref[...]), gssd_ref[...]).astype(BF16)
    mix = _dot(att, w_ref[0:att_w, :]) + _dot(ssd, w_ref[att_w:, :])
    out_ref[...] = x_ref[...] + ga_ref[...] * mix


def _mod_spec(tm, d, rows_per_mod):
    if rows_per_mod == 1:
        return pl.BlockSpec((tm, d), lambda i, *_: (i, 0))
    blocks_per_seq = rows_per_mod // tm
    return pl.BlockSpec((None, 1, d), lambda i, *_: (i // blocks_per_seq, 0, 0))


def _merge_outproj(o_att, y_ssd, z, x, ga1, g_att, g_ssd, w_out, *, tm, rows_per_mod):
    rows, d = x.shape
    att_w = o_att.shape[1]
    row = lambda w: pl.BlockSpec((tm, w), lambda i: (i, 0))
    full = lambda a: pl.BlockSpec(a.shape, lambda i: (0,) * a.ndim)
    return pl.pallas_call(
        functools.partial(_merge_kernel, att_w=att_w),
        grid=(rows // tm,),
        in_specs=[row(att_w), row(att_w), row(att_w), row(d), _mod_spec(tm, d, rows_per_mod),
                  full(g_att), full(g_ssd), full(w_out)],
        out_specs=row(d),
        out_shape=jax.ShapeDtypeStruct((rows, d), F32),
        compiler_params=_cparams(("arbitrary",)),
        name="merge_out_proj",
    )(o_att, y_ssd, z, x, ga1, g_att, g_ssd, w_out)


def _ffn_kernel(x_ref, sh_ref, sc_ref, ga_ref, g_ref, wg_ref, wu_ref, wd_ref, gf_ref, out_ref, h_ref, acc_ref):
    j = pl.program_id(1)

    @pl.when(j == 0)
    def _():
        h_ref[...] = (_rms(x_ref[...], g_ref[...]) * (1.0 + sc_ref[...]) + sh_ref[...]).astype(BF16)
        acc_ref[...] = jnp.zeros(acc_ref.shape, F32)

    h = h_ref[...]
    a = (_silu(_dot(h, wg_ref[...])) * _dot(h, wu_ref[...])).astype(BF16)
    acc_ref[...] += _dot(a, wd_ref[...])

    @pl.when(j == pl.num_programs(1) - 1)
    def _():
        out_ref[...] = _rms(x_ref[...] + ga_ref[...] * acc_ref[...], gf_ref[...])


def _ffn_final(x, sh, sc, ga, g_ffn, w_gate, w_up, w_down, g_final, *, tm, rows_per_mod):
    rows, d = x.shape
    hidden = w_gate.shape[1]
    th = hidden // 2
    mod = _mod_spec(tm, d, rows_per_mod)
    full = lambda a: pl.BlockSpec(a.shape, lambda i, j: (0,) * a.ndim)
    return pl.pallas_call(
        _ffn_kernel,
        grid=(rows // tm, hidden // th),
        in_specs=[pl.BlockSpec((tm, d), lambda i, j: (i, 0)), mod, mod, mod, full(g_ffn),
                  pl.BlockSpec((d, th), lambda i, j: (0, j)),
                  pl.BlockSpec((d, th), lambda i, j: (0, j)),
                  pl.BlockSpec((th, d), lambda i, j: (j, 0)),
                  full(g_final)],
        out_specs=pl.BlockSpec((tm, d), lambda i, j: (i, 0)),
        out_shape=jax.ShapeDtypeStruct((rows, d), F32),
        scratch_shapes=[pltpu.VMEM((tm, d), BF16), pltpu.VMEM((tm, d), F32)],
        compiler_params=_cparams(("arbitrary", "arbitrary")),
        name="ffn_final_norm",
    )(x, sh, sc, ga, g_ffn, w_gate, w_up, w_down, g_final)


def _page_suffix_kernel(x_ref, st_ref, tot_ref):
    x = x_ref[...]
    strict = _tri((LANES, LANES), lambda r, c: r > c).astype(BF16)
    st_ref[...] = _dot3_right(x, strict)
    tot_ref[...] = _dot3_right(x, jnp.ones((LANES, LANES), BF16))


def _page_suffix(lf_t):
    rows = lf_t.shape[0]
    tr = 2048
    spec = pl.BlockSpec((tr, LANES), lambda i: (i, 0))
    return pl.pallas_call(
        _page_suffix_kernel,
        grid=(rows // tr,),
        in_specs=[spec],
        out_specs=[spec, spec],
        out_shape=[jax.ShapeDtypeStruct(lf_t.shape, F32)] * 2,
        compiler_params=_cparams(("arbitrary",)),
        name="page_logf_suffix",
    )(lf_t)


DECODE_PAGES = 8


def _decode_kernel(pt_ref, qbd_ref, kn_ref, vn_ref, cn_ref, *refs):
    n = DECODE_PAGES
    k_refs, v_refs, st_refs, tot_refs = refs[0:n], refs[n:2 * n], refs[2 * n:3 * n], refs[3 * n:4 * n]
    o_ref, m_ref, l_ref, acc_ref, carry_ref = refs[4 * n:]
    j = pl.program_id(1)
    att_w = N_HEADS * HEAD_DIM

    @pl.when(j == 0)
    def _():
        m_ref[...] = jnp.full(m_ref.shape, NEG_BIG, F32)
        l_ref[...] = jnp.zeros(l_ref.shape, F32)
        acc_ref[...] = jnp.zeros(acc_ref.shape, F32)
        carry_ref[...] = cn_ref[...]

    qbd = qbd_ref[...]
    carry = carry_ref[...]
    s_parts = [None] * n
    for i in reversed(range(n)):
        kt = k_refs[i][...].reshape(att_w, CHUNK).astype(BF16)
        s_parts[i] = _dot(qbd, kt) + (st_refs[i][...] + carry)
        carry = carry + tot_refs[i][...]
    carry_ref[...] = carry
    s = jnp.concatenate(s_parts, axis=1)
    m_old = m_ref[...]
    m_new = jnp.maximum(m_old, jnp.max(s, axis=1, keepdims=True))
    alpha = jnp.exp(m_old - m_new)
    p = jnp.exp(s - m_new)
    l_ref[...] = alpha * l_ref[...] + jnp.sum(p, axis=1, keepdims=True)
    m_ref[...] = m_new
    pb = p.astype(BF16)
    pv = jnp.zeros(acc_ref.shape, F32)
    for i in range(n):
        vt = v_refs[i][...].reshape(att_w, CHUNK).astype(BF16)
        pv = pv + lax.dot_general(pb[:, i * CHUNK:(i + 1) * CHUNK], vt, NT_DIMS, preferred_element_type=F32)
    acc_ref[...] = alpha * acc_ref[...] + pv

    @pl.when(j == pl.num_programs(1) - 1)
    def _():
        s_new = jnp.sum(qbd.astype(F32) * kn_ref[...].astype(F32), axis=1, keepdims=True)
        m_fin = jnp.maximum(m_ref[...], s_new)
        a_fin = jnp.exp(m_ref[...] - m_fin)
        p_new = jnp.exp(s_new - m_fin)
        num = a_fin * acc_ref[...] + p_new.astype(BF16).astype(F32) * vn_ref[...].astype(F32)
        o8 = num / (a_fin * l_ref[...] + p_new)
        own = (lax.broadcasted_iota(jnp.int32, o8.shape, 1) // HEAD_DIM
               == lax.broadcasted_iota(jnp.int32, o8.shape, 0))
        o_ref[...] = jnp.sum(jnp.where(own, o8, 0.0), axis=0, keepdims=True)


def _decode_attention(page_table, qbd, kn, vn, cn_bc, cache_kt, cache_vt, st, tot):
    dec_b, n_pages = page_table.shape
    n = DECODE_PAGES
    n_steps = n_pages // n
    att_w = N_HEADS * HEAD_DIM

    def per_b(shape):
        return pl.BlockSpec((None,) + shape, lambda b, j, pt: (b,) + (0,) * len(shape))

    def paged(shape, i):
        return pl.BlockSpec((None,) + shape,
                            lambda b, j, pt: (pt[b, (n_steps - 1 - j) * n + i],) + (0,) * len(shape))

    kv_specs = [paged((N_HEADS, HEAD_DIM, CHUNK), i) for i in range(n)]
    lf_specs = [paged((N_HEADS, CHUNK), i) for i in range(n)]
    return pl.pallas_call(
        _decode_kernel,
        grid_spec=pltpu.PrefetchScalarGridSpec(
            num_scalar_prefetch=1,
            grid=(dec_b, n_steps),
            in_specs=[per_b((N_HEADS, att_w)), per_b((1, att_w)), per_b((1, att_w)), per_b((N_HEADS, CHUNK))]
                     + kv_specs + kv_specs + lf_specs + lf_specs,
            out_specs=per_b((1, att_w)),
            scratch_shapes=[pltpu.VMEM((N_HEADS, 1), F32), pltpu.VMEM((N_HEADS, 1), F32),
                            pltpu.VMEM((N_HEADS, att_w), F32), pltpu.VMEM((N_HEADS, CHUNK), F32)]),
        out_shape=jax.ShapeDtypeStruct((dec_b, 1, att_w), F32),
        compiler_params=_cparams(("arbitrary", "arbitrary")),
        name="fox_decode_attention",
    )(page_table, qbd, kn, vn, cn_bc, *([cache_kt] * n), *([cache_vt] * n), *([st] * n), *([tot] * n))


def _sample_conv_kernel(s0_ref, s1_ref, s2_ref, x_ref, sa_ref, cw_ref, cb_ref, alog_row_ref, u_ref, dec_ref):
    w = cw_ref[...]
    acc = cb_ref[...] + s0_ref[...] * w[0:1]
    acc = acc + s1_ref[...] * w[1:2]
    acc = acc + s2_ref[...] * w[2:3]
    acc = acc + x_ref[...] * w[3:4]
    u_ref[...] = _silu(acc)
    dec_ref[...] = jnp.exp(sa_ref[...] * (-jnp.exp(alog_row_ref[...])))


def _sample_conv(s0, s1, s2, xbc, small_act, conv_w, conv_b, alog_row):
    return pl.pallas_call(
        _sample_conv_kernel,
        out_shape=[jax.ShapeDtypeStruct(xbc.shape, F32), jax.ShapeDtypeStruct(small_act.shape, F32)],
        name="sample_conv",
    )(s0, s1, s2, xbc, small_act, conv_w, conv_b, alog_row)


SSD_STEP_ROWS = 8


def _ssd_step_kernel(dt_ref, dec_ref, dsk_ref, h_ref, xcol_ref, bc_ref, hn_ref, ycol_ref):
    gn = (N_HEADS // HEADS_PER_GROUP) * D_STATE
    for r in range(SSD_STEP_ROWS):
        b = pl.program_id(0) * SSD_STEP_ROWS + r
        for h in range(N_HEADS):
            g = h // HEADS_PER_GROUP
            xc = xcol_ref[r, h * HEAD_DIM:(h + 1) * HEAD_DIM, :]
            b_row = bc_ref[r, :, g * D_STATE:(g + 1) * D_STATE]
            c_row = bc_ref[r, :, gn + g * D_STATE:gn + (g + 1) * D_STATE]
            hn = dec_ref[b, h] * h_ref[r, h] + (dt_ref[b, h] * xc) * b_row
            hn_ref[r, h] = hn
            ycol_ref[r, h * HEAD_DIM:(h + 1) * HEAD_DIM, :] = (
                jnp.sum(hn * c_row, axis=1, keepdims=True) + dsk_ref[0, h] * xc)


def _ssd_step(dt, dec, d_skip, state, xcol, bc):
    dec_b = state.shape[0]
    rows = SSD_STEP_ROWS
    smem = pl.BlockSpec(memory_space=pltpu.SMEM)
    st_spec = pl.BlockSpec((rows,) + state.shape[1:], lambda b: (b, 0, 0, 0))
    col_spec = pl.BlockSpec((rows,) + xcol.shape[1:], lambda b: (b, 0, 0))
    return pl.pallas_call(
        _ssd_step_kernel,
        grid=(dec_b // rows,),
        in_specs=[smem, smem, smem, st_spec, col_spec,
                  pl.BlockSpec((rows,) + bc.shape[1:], lambda b: (b, 0, 0))],
        out_specs=[st_spec, col_spec],
        out_shape=[jax.ShapeDtypeStruct(state.shape, F32), jax.ShapeDtypeStruct(xcol.shape, F32)],
        compiler_params=_cparams(("arbitrary",)),
        name="ssd_step",
    )(dt, dec, d_skip, state, xcol, bc)


def kernel(x_prompt, x_sample, c_prompt, c_sample, cache_k, cache_v, cache_logf, state_conv, state_ssm, page_table, w_ada, b_ada, g_mix, w_in, b_f, conv_w, conv_b, dt_bias, A_log, D_skip, g_att_out, g_ssd_out, w_out, g_ffn, w_gate, w_up, w_down, g_final):
    batch, seq, d = x_prompt.shape
    dec_b = x_sample.shape[0]
    att_w = N_HEADS * HEAD_DIM
    ssd_w = g_ssd_out.shape[1]
    conv_dim = conv_w.shape[2]
    n_pool = cache_k.shape[1]
    l = 0

    wi = w_in[l]
    o_f = 3 * att_w
    o_z = o_f + N_HEADS
    o_x = o_z + ssd_w
    o_dt = o_x + conv_dim
    w_big = jnp.concatenate([wi[:, :o_f], wi[:, o_z:o_dt]], axis=1).astype(BF16)
    pad = jnp.zeros((d, LANES - 2 * N_HEADS), F32)
    w_small = jnp.concatenate([wi[:, o_f:o_z], wi[:, o_dt:], pad], axis=1).astype(BF16)
    zpad = jnp.zeros((LANES - 2 * N_HEADS,), F32)
    bias_small = jnp.concatenate([b_f[l], dt_bias[l], zpad])[None, :]
    alog_row = jnp.concatenate([jnp.zeros((N_HEADS,), F32), A_log[l], zpad])[None, :]
    alog_col = A_log[l][:, None]
    d_exp = jnp.repeat(D_skip[l], HEAD_DIM)[None, :]
    w_out_b = w_out[l].astype(BF16)
    wg_b, wu_b, wd_b = w_gate[l].astype(BF16), w_up[l].astype(BF16), w_down[l].astype(BF16)
    g_final2 = g_final[None, :]

    ada_rows = 144
    c_all = jnp.concatenate([c_prompt, c_sample, jnp.zeros((ada_rows - batch - dec_b, d), F32)], axis=0)
    mod = _ada(c_all, w_ada[l], b_ada[l][None, :])
    mod_p = mod[:batch].reshape(batch, 6, 1, d)
    sh1_p, sc1_p, ga1_p, sh2_p, sc2_p, ga2_p = (mod_p[:, i] for i in range(6))
    mod_s = mod[batch:batch + dec_b].reshape(dec_b, 6, d)
    sh1_s, sc1_s, ga1_s, sh2_s, sc2_s, ga2_s = (mod_s[:, i] for i in range(6))

    xp = x_prompt.reshape(batch * seq, d)
    tm_p = 512
    qb, k_p, v_p, kb, vb, z_p, xbc_p, sa_p = _inproj(
        xp, sh1_p, sc1_p, g_mix[l][None, :], w_big, w_small, bias_small, tm=tm_p, rows_per_mod=seq)
    logf_p = sa_p[:, :N_HEADS].reshape(batch, seq, N_HEADS)
    neg_f = _neg_cumsum(jnp.swapaxes(logf_p, 1, 2).reshape(batch * N_HEADS, seq))
    o_att_p = _prompt_attention(qb, kb, vb, neg_f.reshape(batch * N_HEADS // 2, 2, seq), batch=batch, seq=seq)
    y_ssd_p, ht_p = _ssd_prompt(xbc_p, sa_p, conv_w[l], conv_b[l][None, :], alog_row, alog_col, d_exp,
                                batch=batch, seq=seq)
    x1_p = _merge_outproj(o_att_p, y_ssd_p, z_p, xp, ga1_p, g_att_out[l][None, :], g_ssd_out[l][None, :],
                          w_out_b, tm=tm_p, rows_per_mod=seq)
    y_p = _ffn_final(x1_p, sh2_p, sc2_p, ga2_p, g_ffn[l][None, :], wg_b, wu_b, wd_b, g_final2,
                     tm=tm_p, rows_per_mod=seq)

    xs = x_sample.reshape(dec_b, d)
    qs, k_s, v_s, ks_b, vs_b, z_s, xbc_s, sa_s = _inproj(
        xs, sh1_s, sc1_s, g_mix[l][None, :], w_big, w_small, bias_small, tm=dec_b, rows_per_mod=1)
    lf_t = jnp.swapaxes(cache_logf[l], 1, 2).reshape(n_pool * N_HEADS, CHUNK)
    cache_kt = jnp.transpose(cache_k[l], (0, 2, 3, 1))
    cache_vt = jnp.transpose(cache_v[l], (0, 2, 3, 1))
    st, tot = _page_suffix(lf_t)
    cn_bc = jnp.broadcast_to(sa_s[:, :N_HEADS, None], (dec_b, N_HEADS, CHUNK))
    own_head = jnp.eye(N_HEADS, dtype=bool)[None, :, :, None]
    qbd = jnp.where(own_head, qs.reshape(dec_b, 1, N_HEADS, HEAD_DIM), jnp.zeros((), BF16))
    o_att_s = _decode_attention(
        page_table, qbd.reshape(dec_b, N_HEADS, att_w), ks_b[:, None, :], vs_b[:, None, :], cn_bc,
        cache_kt, cache_vt,
        st.reshape(n_pool, N_HEADS, CHUNK), tot.reshape(n_pool, N_HEADS, CHUNK)).reshape(dec_b, att_w)
    sc_l = state_conv[l]
    u_s, dec_s = _sample_conv(sc_l[:, 0], sc_l[:, 1], sc_l[:, 2], xbc_s, sa_s, conv_w[l], conv_b[l][None, :], alog_row)
    dt_s = sa_s[:, SMALL_DT0:SMALL_DT0 + N_HEADS]
    h_new, ycol = _ssd_step(dt_s, dec_s[:, SMALL_DT0:SMALL_DT0 + N_HEADS], D_skip[l][None, :], state_ssm[l],
                            u_s[:, :ssd_w].reshape(dec_b, ssd_w, 1), u_s[:, ssd_w:].reshape(dec_b, 1, conv_dim - ssd_w))
    x1_s = _merge_outproj(o_att_s, ycol.reshape(dec_b, ssd_w), z_s, xs, ga1_s, g_att_out[l][None, :],
                          g_ssd_out[l][None, :], w_out_b, tm=dec_b, rows_per_mod=1)
    y_s = _ffn_final(x1_s, sh2_s, sc2_s, ga2_s, g_ffn[l][None, :], wg_b, wu_b, wd_b, g_final2,
                     tm=dec_b, rows_per_mod=1)

    ht = ht_p.reshape(batch, N_HEADS // 2, D_STATE, 2, HEAD_DIM)
    ssm_prompt = jnp.transpose(ht, (0, 1, 3, 4, 2)).reshape(1, batch, N_HEADS, HEAD_DIM, D_STATE)
    conv_prompt = xbc_p.reshape(batch, seq, conv_dim)[:, seq - 3:, :][None]
    conv_sample = jnp.concatenate([sc_l[:, 1:], xbc_s[:, None, :]], axis=1)[None]
    return (y_p.reshape(batch, seq, d), y_s.reshape(dec_b, 1, d),
            k_p.reshape(1, batch, seq, N_HEADS, HEAD_DIM), v_p.reshape(1, batch, seq, N_HEADS, HEAD_DIM),
            logf_p[None],
            conv_prompt, ssm_prompt,
            k_s.reshape(1, dec_b, 1, N_HEADS, HEAD_DIM), v_s.reshape(1, dec_b, 1, N_HEADS, HEAD_DIM),
            sa_s[:, :N_HEADS].reshape(1, dec_b, 1, N_HEADS),
            conv_sample, h_new[None])
```

```python
import functools

import jax
import jax.numpy as jnp
from jax import lax
from jax.experimental import pallas as pl
from jax.experimental.pallas import tpu as pltpu

F32 = jnp.float32
BF16 = jnp.bfloat16

HEAD_DIM = 64
N_HEADS = 8
HEADS_PER_GROUP = 4
D_STATE = 128
CHUNK = 128
NORM_EPS = 1e-6
ATT_SCALE = HEAD_DIM ** -0.5
LANES = 128
SMALL_DT0 = 8
NEG_BIG = -1e30
VMEM_LIMIT = 56 * 1024 * 1024

NT_DIMS = (((1,), (1,)), ((), ()))


def _cparams(sem):
    return pltpu.CompilerParams(dimension_semantics=sem, vmem_limit_bytes=VMEM_LIMIT)


def _silu(x):
    return x * jax.nn.sigmoid(x)


def _softplus(x):
    return jnp.maximum(x, 0.0) + jnp.log1p(jnp.exp(-jnp.abs(x)))


def _rms(x, g):
    return x * lax.rsqrt(jnp.mean(x * x, axis=-1, keepdims=True) + NORM_EPS) * g


def _split3(x):
    hi = x.astype(BF16)
    r1 = x - hi.astype(F32)
    mid = r1.astype(BF16)
    lo = (r1 - mid.astype(F32)).astype(BF16)
    return hi, mid, lo


def _dot(a, b):
    return jnp.dot(a, b, preferred_element_type=F32)


def _dot3_right(x, m):
    hi, mid, lo = _split3(x)
    return _dot(hi, m) + _dot(mid, m) + _dot(lo, m)


def _dot3_left(m, x):
    hi, mid, lo = _split3(x)
    return _dot(m, hi) + _dot(m, mid) + _dot(m, lo)


def _tri(shape, cmp):
    r = lax.broadcasted_iota(jnp.int32, shape, 0)
    c = lax.broadcasted_iota(jnp.int32, shape, 1)
    return cmp(r, c)


def _ada_kernel(c_ref, w_ref, b_ref, o_ref):
    s = _silu(c_ref[...]).astype(BF16)
    o_ref[...] = _dot(s, w_ref[...].astype(BF16)) + b_ref[...]


def _ada(c_all, w_ada, b_ada):
    rows, d = c_all.shape
    n = w_ada.shape[1]
    tn = 512
    return pl.pallas_call(
        _ada_kernel,
        grid=(n // tn,),
        in_specs=[pl.BlockSpec((rows, d), lambda j: (0, 0)),
                  pl.BlockSpec((d, tn), lambda j: (0, j)),
                  pl.BlockSpec((1, tn), lambda j: (0, j))],
        out_specs=pl.BlockSpec((rows, tn), lambda j: (0, j)),
        out_shape=jax.ShapeDtypeStruct((rows, n), F32),
        compiler_params=_cparams(("arbitrary",)),
        name="ada_terms",
    )(c_all, w_ada, b_ada)


def _inproj_kernel(x_ref, sh_ref, sc_ref, g_ref, wb_ref, ws_ref, bias_ref,
                   q_ref, k_ref, v_ref, kb_ref, vb_ref, z_ref, xbc_ref, small_ref, qt_ref, vt_ref, *, att_w):
    h = (_rms(x_ref[...], g_ref[...]) * (1.0 + sc_ref[...]) + sh_ref[...]).astype(BF16)
    a = att_w
    q = _dot(h, wb_ref[:, 0:a]) * ATT_SCALE
    q_ref[...] = q.astype(BF16)
    qt_ref[...] = q.T.astype(BF16)
    k = _dot(h, wb_ref[:, a:2 * a])
    k_ref[...] = k
    kb_ref[...] = k.astype(BF16)
    v = _dot(h, wb_ref[:, 2 * a:3 * a])
    v_ref[...] = v
    vb_ref[...] = v.astype(BF16)
    vt_ref[...] = v.T.astype(BF16)
    z_ref[...] = _dot(h, wb_ref[:, 3 * a:4 * a])
    xbc_ref[...] = _dot(h, wb_ref[:, 4 * a:])
    s = _dot(h, ws_ref[...]) + bias_ref[...]
    lane = lax.broadcasted_iota(jnp.int32, s.shape, 1)
    small_ref[...] = jnp.where(lane < SMALL_DT0, -_softplus(-s), _softplus(s))


def _inproj(x, sh, sc, g_mix, w_big, w_small, bias_small, *, tm, rows_per_mod):
    rows, d = x.shape
    att_w = N_HEADS * HEAD_DIM
    conv_dim = w_big.shape[1] - 4 * att_w
    if rows_per_mod == 1:
        mod_spec = pl.BlockSpec((tm, d), lambda i: (i, 0))
    else:
        blocks_per_seq = rows_per_mod // tm
        mod_spec = pl.BlockSpec((None, 1, d), lambda i: (i // blocks_per_seq, 0, 0))
    row = lambda w: pl.BlockSpec((tm, w), lambda i: (i, 0))
    full = lambda a: pl.BlockSpec(a.shape, lambda i: (0,) * a.ndim)
    sds = lambda w, dt: jax.ShapeDtypeStruct((rows, w), dt)
    return pl.pallas_call(
        functools.partial(_inproj_kernel, att_w=att_w),
        grid=(rows // tm,),
        in_specs=[row(d), mod_spec, mod_spec, full(g_mix), full(w_big), full(w_small), full(bias_small)],
        out_specs=[row(att_w)] * 6 + [row(conv_dim), row(LANES)] + [pl.BlockSpec((att_w, tm), lambda i: (0, i))] * 2,
        out_shape=[sds(att_w, BF16), sds(att_w, F32), sds(att_w, F32), sds(att_w, BF16), sds(att_w, BF16),
                   sds(att_w, F32), sds(conv_dim, F32), sds(LANES, F32)]
                  + [jax.ShapeDtypeStruct((att_w, rows), BF16)] * 2,
        compiler_params=_cparams(("arbitrary",)),
        name="in_proj",
    )(x, sh, sc, g_mix, w_big, w_small, bias_small)


def _cumsum_kernel(x_ref, o_ref):
    n_blocks = x_ref.shape[1] // LANES
    upper = _tri((LANES, LANES), lambda r, c: r <= c).astype(BF16)
    carry = jnp.zeros((x_ref.shape[0], 1), F32)
    for c in range(n_blocks):
        cs = _dot3_right(x_ref[:, c * LANES:(c + 1) * LANES], upper) + carry
        o_ref[:, c * LANES:(c + 1) * LANES] = -cs
        carry = cs[:, LANES - 1:LANES]


def _neg_cumsum(x):
    return pl.pallas_call(
        _cumsum_kernel,
        out_shape=jax.ShapeDtypeStruct(x.shape, F32),
        name="logf_cumsum",
    )(x)


ATTN_TK = 512
ATTN_QS = 128


def _attn_kernel(qt_ref, k_ref, vt_ref, nf_ref, o_ref, *, tq):
    qi = pl.program_id(2)
    qt = qt_ref[...]
    row_q = lax.broadcasted_iota(jnp.int32, qt.shape, 0)
    qt_heads = (jnp.where(row_q < HEAD_DIM, qt, jnp.zeros_like(qt)),
                jnp.where(row_q >= HEAD_DIM, qt, jnp.zeros_like(qt)))

    tk = ATTN_TK
    qs = ATTN_QS
    n_strips = tq // qs

    def step(off, carry, diag):
        kblk = k_ref[pl.ds(off, tk), :]
        vtblk = vt_ref[:, pl.ds(off, tk)]
        out = []
        for hh in range(2):
            bias = nf_ref[hh, pl.ds(off, tk), :]
            ms, ls, accs = carry[hh]
            new_m, new_l, new_acc = [], [], []
            for c in range(n_strips):
                m, l, acc = ms[c], ls[c], accs[c]
                first_q = c * qs
                if diag is not None and diag * tk > first_q + qs - 1:
                    new_m.append(m), new_l.append(l), new_acc.append(acc)
                    continue
                s = _dot(kblk, qt_heads[hh][:, first_q:first_q + qs]) + bias
                if diag is not None and diag * tk + tk - 1 > first_q:
                    key = lax.broadcasted_iota(jnp.int32, s.shape, 0) + diag * tk
                    qry = lax.broadcasted_iota(jnp.int32, s.shape, 1) + first_q
                    s = jnp.where(key <= qry, s, -jnp.inf)
                m_new = jnp.maximum(m, jnp.max(s, axis=0, keepdims=True))
                alpha = jnp.exp(m - m_new)
                p = jnp.exp(s - m_new)
                new_m.append(m_new)
                new_l.append(alpha * l + jnp.sum(p, axis=0, keepdims=True))
                new_acc.append(alpha * acc + _dot(vtblk, p.astype(BF16)))
            out.append((tuple(new_m), tuple(new_l), tuple(new_acc)))
        return tuple(out)

    init = tuple((tuple(jnp.full((1, qs), NEG_BIG, F32) for _ in range(n_strips)),
                  tuple(jnp.zeros((1, qs), F32) for _ in range(n_strips)),
                  tuple(jnp.zeros((LANES, qs), F32) for _ in range(n_strips))) for _ in range(2))
    carry = lax.fori_loop(0, qi * (tq // tk), lambda ki, c: step(pl.multiple_of(ki * tk, tk), c, None), init)
    for dblk in range(tq // tk):
        carry = step(pl.multiple_of(qi * tq + dblk * tk, tk), carry, dblk)
    (_, l_a, acc_a), (_, l_b, acc_b) = carry
    row_o = lax.broadcasted_iota(jnp.int32, (LANES, qs), 0)
    for c in range(n_strips):
        o_t = jnp.where(row_o < HEAD_DIM, acc_a[c] / l_a[c], acc_b[c] / l_b[c])
        o_ref[c * qs:(c + 1) * qs, :] = o_t.T


def _prompt_attention(qt, kb, vt, neg_f, *, batch, seq):
    tq = 512
    nq = seq // tq
    n_pairs = N_HEADS // 2
    return pl.pallas_call(
        functools.partial(_attn_kernel, tq=tq),
        grid=(batch, n_pairs, nq),
        in_specs=[pl.BlockSpec((LANES, tq), lambda b, p, i: (p, b * nq + i)),
                  pl.BlockSpec((seq, LANES), lambda b, p, i: (b, p)),
                  pl.BlockSpec((LANES, seq), lambda b, p, i: (p, b)),
                  pl.BlockSpec((None, 2, seq, 1), lambda b, p, i: (b * n_pairs + p, 0, 0, 0))],
        out_specs=pl.BlockSpec((tq, LANES), lambda b, p, i: (b * nq + i, p)),
        out_shape=jax.ShapeDtypeStruct(kb.shape, F32),
        compiler_params=_cparams(("arbitrary", "arbitrary", "arbitrary")),
        name="fox_prompt_attention",
    )(qt, kb, vt, neg_f)


def _ssd_prompt_kernel(xbc_ref, sa_ref, cw_ref, cb_ref, alog_row_ref, alog_col_ref, dexp_ref,
                       y_ref, ht_ref, ext_ref, hts_ref, *, ssd_w):
    c = pl.program_id(1)
    q_len = CHUNK
    tail = 8

    @pl.when(c == 0)
    def _():
        ext_ref[0:tail, :] = jnp.zeros((tail, ext_ref.shape[1]), F32)
        hts_ref[...] = jnp.zeros(hts_ref.shape, F32)

    x = xbc_ref[...]
    ext_ref[tail:tail + q_len, :] = x
    w = cw_ref[...]
    acc = cb_ref[...] + ext_ref[tail - 3:tail - 3 + q_len, :] * w[0:1]
    acc = acc + ext_ref[tail - 2:tail - 2 + q_len, :] * w[1:2]
    acc = acc + ext_ref[tail - 1:tail - 1 + q_len, :] * w[2:3]
    acc = acc + x * w[3:4]
    u = _silu(acc)
    ext_ref[0:tail, :] = x[q_len - tail:q_len, :]

    sa = sa_ref[...]
    lane = lax.broadcasted_iota(jnp.int32, (1, LANES), 1)
    dt_lanes = (lane >= SMALL_DT0) & (lane < SMALL_DT0 + N_HEADS)
    a_row = jnp.where(dt_lanes, -jnp.exp(alog_row_ref[...]), 0.0)
    lower = _tri((q_len, q_len), lambda r, c_: r >= c_)
    cs_full = _dot3_left(lower.astype(BF16), sa * a_row)
    dt_t = sa.T[SMALL_DT0:SMALL_DT0 + N_HEADS, :]
    da_t = dt_t * (-jnp.exp(alog_col_ref[...]))
    cs_t = _dot3_right(da_t, _tri((q_len, q_len), lambda r, c_: r <= c_).astype(BF16))
    tot_full = jnp.broadcast_to(cs_full[q_len - 1:q_len, :], cs_full.shape)

    lane_p = lax.broadcasted_iota(jnp.int32, (q_len, LANES), 1)
    first = lane_p < HEAD_DIM
    n_groups = N_HEADS // HEADS_PER_GROUP
    gn = n_groups * D_STATE
    for g in range(n_groups):
        bg = u[:, ssd_w + g * D_STATE: ssd_w + (g + 1) * D_STATE]
        cg = u[:, ssd_w + gn + g * D_STATE: ssd_w + gn + (g + 1) * D_STATE].astype(BF16)
        cb = lax.dot_general(cg, bg.astype(BF16), NT_DIMS, preferred_element_type=F32)
        bg_t = bg.T.astype(BF16)
        for k in range(HEADS_PER_GROUP // 2):
            pr = g * (HEADS_PER_GROUP // 2) + k
            x_pair = u[:, pr * LANES:(pr + 1) * LANES]
            x_pair_b = x_pair.astype(BF16)
            ys, cols, tots, dts = [], [], [], []
            for h in (2 * pr, 2 * pr + 1):
                col = cs_full[:, SMALL_DT0 + h:SMALL_DT0 + h + 1]
                seg = col - cs_t[h:h + 1, :]
                lmat = jnp.exp(jnp.where(lower, seg, -jnp.inf))
                mh = (cb * lmat * dt_t[h:h + 1, :]).astype(BF16)
                ys.append(_dot(mh, x_pair_b))
                cols.append(col)
                tots.append(tot_full[:, SMALL_DT0 + h:SMALL_DT0 + h + 1])
                dts.append(sa[:, SMALL_DT0 + h:SMALL_DT0 + h + 1])
            y_diag = jnp.where(first, ys[0], ys[1])
            h_prev = hts_ref[pr]
            e_col = jnp.where(first, jnp.exp(cols[0]), jnp.exp(cols[1]))
            y_off = _dot(cg, h_prev.astype(BF16)) * e_col
            y_ref[:, pr * LANES:(pr + 1) * LANES] = (
                y_diag + y_off + dexp_ref[:, pr * LANES:(pr + 1) * LANES] * x_pair)
            w_pair = jnp.where(first, jnp.exp(tots[0] - cols[0]) * dts[0], jnp.exp(tots[1] - cols[1]) * dts[1])
            new_t = _dot(bg_t, (x_pair * w_pair).astype(BF16))
            dec = jnp.where(first, jnp.exp(tots[0]), jnp.exp(tots[1]))
            hts_ref[pr] = h_prev * dec + new_t

    @pl.when(c == pl.num_programs(1) - 1)
    def _():
        ht_ref[...] = hts_ref[...]


def _ssd_prompt(xbc, small_act, conv_w, conv_b, alog_row, alog_col, d_exp, *, batch, seq):
    ssd_w = d_exp.shape[1]
    conv_dim = xbc.shape[1]
    nc = seq // CHUNK
    n_pairs = N_HEADS // 2
    full = lambda a: pl.BlockSpec(a.shape, lambda b, c: (0,) * a.ndim)
    return pl.pallas_call(
        functools.partial(_ssd_prompt_kernel, ssd_w=ssd_w),
        grid=(batch, nc),
        in_specs=[pl.BlockSpec((CHUNK, conv_dim), lambda b, c: (b * nc + c, 0)),
                  pl.BlockSpec((CHUNK, LANES), lambda b, c: (b * nc + c, 0)),
                  full(conv_w), full(conv_b), full(alog_row), full(alog_col), full(d_exp)],
        out_specs=[pl.BlockSpec((CHUNK, ssd_w), lambda b, c: (b * nc + c, 0)),
                   pl.BlockSpec((None, n_pairs, D_STATE, LANES), lambda b, c: (b, 0, 0, 0))],
        out_shape=[jax.ShapeDtypeStruct((batch * seq, ssd_w), F32),
                   jax.ShapeDtypeStruct((batch, n_pairs, D_STATE, LANES), F32)],
        scratch_shapes=[pltpu.VMEM((CHUNK + 8, conv_dim), F32),
                        pltpu.VMEM((n_pairs, D_STATE, LANES), F32)],
        compiler_params=_cparams(("arbitrary", "arbitrary")),
        name="ssd_prompt",
    )(xbc, small_act, conv_w, conv_b, alog_row, alog_col, d_exp)


def _merge_kernel(o_ref, y_ref, z_ref, x_ref, ga_ref, gatt_ref, gssd_ref, w_ref, out_ref, *, att_w):
    att = _rms(o_ref[...], gatt_ref[...]).astype(BF16)
    ssd = _rms(y_ref[...] * _silu(z_ref[...]), gssd_ref[...]).astype(BF16)
    mix = _dot(att, w_ref[0:att_w, :]) + _dot(ssd, w_ref[att_w:, :])
    out_ref[...] = x_ref[...] + ga_ref[...] * mix


def _mod_spec(tm, d, rows_per_mod):
    if rows_per_mod == 1:
        return pl.BlockSpec((tm, d), lambda i, *_: (i, 0))
    blocks_per_seq = rows_per_mod // tm
    return pl.BlockSpec((None, 1, d), lambda i, *_: (i // blocks_per_seq, 0, 0))


def _merge_outproj(o_att, y_ssd, z, x, ga1, g_att, g_ssd, w_out, *, tm, rows_per_mod):
    rows, d = x.shape
    att_w = o_att.shape[1]
    row = lambda w: pl.BlockSpec((tm, w), lambda i: (i, 0))
    full = lambda a: pl.BlockSpec(a.shape, lambda i: (0,) * a.ndim)
    return pl.pallas_call(
        functools.partial(_merge_kernel, att_w=att_w),
        grid=(rows // tm,),
        in_specs=[row(att_w), row(att_w), row(att_w), row(d), _mod_spec(tm, d, rows_per_mod),
                  full(g_att), full(g_ssd), full(w_out)],
        out_specs=row(d),
        out_shape=jax.ShapeDtypeStruct((rows, d), F32),
        compiler_params=_cparams(("arbitrary",)),
        name="merge_out_proj",
    )(o_att, y_ssd, z, x, ga1, g_att, g_ssd, w_out)


def _ffn_kernel(x_ref, sh_ref, sc_ref, ga_ref, g_ref, wg_ref, wu_ref, wd_ref, gf_ref, out_ref, h_ref, acc_ref):
    j = pl.program_id(1)

    @pl.when(j == 0)
    def _():
        h_ref[...] = (_rms(x_ref[...], g_ref[...]) * (1.0 + sc_ref[...]) + sh_ref[...]).astype(BF16)
        acc_ref[...] = jnp.zeros(acc_ref.shape, F32)

    h = h_ref[...]
    a = (_silu(_dot(h, wg_ref[...])) * _dot(h, wu_ref[...])).astype(BF16)
    acc_ref[...] += _dot(a, wd_ref[...])

    @pl.when(j == pl.num_programs(1) - 1)
    def _():
        out_ref[...] = _rms(x_ref[...] + ga_ref[...] * acc_ref[...], gf_ref[...])


def _ffn_final(x, sh, sc, ga, g_ffn, w_gate, w_up, w_down, g_final, *, tm, rows_per_mod):
    rows, d = x.shape
    hidden = w_gate.shape[1]
    th = hidden // 2
    mod = _mod_spec(tm, d, rows_per_mod)
    full = lambda a: pl.BlockSpec(a.shape, lambda i, j: (0,) * a.ndim)
    return pl.pallas_call(
        _ffn_kernel,
        grid=(rows // tm, hidden // th),
        in_specs=[pl.BlockSpec((tm, d), lambda i, j: (i, 0)), mod, mod, mod, full(g_ffn),
                  pl.BlockSpec((d, th), lambda i, j: (0, j)),
                  pl.BlockSpec((d, th), lambda i, j: (0, j)),
                  pl.BlockSpec((th, d), lambda i, j: (j, 0)),
                  full(g_final)],
        out_specs=pl.BlockSpec((tm, d), lambda i, j: (i, 0)),
        out_shape=jax.ShapeDtypeStruct((rows, d), F32),
        scratch_shapes=[pltpu.VMEM((tm, d), BF16), pltpu.VMEM((tm, d), F32)],
        compiler_params=_cparams(("arbitrary", "arbitrary")),
        name="ffn_final_norm",
    )(x, sh, sc, ga, g_ffn, w_gate, w_up, w_down, g_final)


def _page_suffix_kernel(x_ref, o_ref):
    x = x_ref[...]
    strict = _tri((LANES, LANES), lambda r, c: r > c).astype(BF16)
    o_ref[:, 0:LANES] = _dot3_right(x, strict)
    o_ref[:, LANES:2 * LANES] = _dot3_right(x, jnp.ones((LANES, LANES), BF16))


def _page_suffix(lf_t):
    rows = lf_t.shape[0]
    tr = 2048
    return pl.pallas_call(
        _page_suffix_kernel,
        grid=(rows // tr,),
        in_specs=[pl.BlockSpec((tr, LANES), lambda i: (i, 0))],
        out_specs=pl.BlockSpec((tr, 2 * LANES), lambda i: (i, 0)),
        out_shape=jax.ShapeDtypeStruct((rows, 2 * LANES), F32),
        compiler_params=_cparams(("arbitrary",)),
        name="page_logf_suffix",
    )(lf_t)


DECODE_PAGES = 16


def _decode_kernel(pt_ref, qbd_ref, kn_ref, vn_ref, cn_ref, *refs):
    n = DECODE_PAGES
    k_refs, v_refs, lf_refs = refs[0:n], refs[n:2 * n], refs[2 * n:3 * n]
    o_ref, m_ref, l_ref, acc_ref, carry_ref = refs[3 * n:]
    j = pl.program_id(1)
    att_w = N_HEADS * HEAD_DIM

    @pl.when(j == 0)
    def _():
        m_ref[...] = jnp.full(m_ref.shape, NEG_BIG, F32)
        l_ref[...] = jnp.zeros(l_ref.shape, F32)
        acc_ref[...] = jnp.zeros(acc_ref.shape, F32)
        carry_ref[...] = cn_ref[...]

    qbd = qbd_ref[...]
    carry = carry_ref[...]
    s_parts = [None] * n
    for i in reversed(range(n)):
        kt = k_refs[i][...].reshape(att_w, CHUNK).astype(BF16)
        s_parts[i] = _dot(qbd, kt) + (lf_refs[i][:, 0:CHUNK] + carry)
        carry = carry + lf_refs[i][:, CHUNK:2 * CHUNK]
    carry_ref[...] = carry
    s = jnp.concatenate(s_parts, axis=1)
    m_old = m_ref[...]
    m_new = jnp.maximum(m_old, jnp.max(s, axis=1, keepdims=True))
    alpha = jnp.exp(m_old - m_new)
    p = jnp.exp(s - m_new)
    l_ref[...] = alpha * l_ref[...] + jnp.sum(p, axis=1, keepdims=True)
    m_ref[...] = m_new
    pb = p.astype(BF16)
    pv = jnp.zeros(acc_ref.shape, F32)
    for i in range(n):
        vt = v_refs[i][...].reshape(att_w, CHUNK).astype(BF16)
        pv = pv + lax.dot_general(pb[:, i * CHUNK:(i + 1) * CHUNK], vt, NT_DIMS, preferred_element_type=F32)
    acc_ref[...] = alpha * acc_ref[...] + pv

    @pl.when(j == pl.num_programs(1) - 1)
    def _():
        s_new = jnp.sum(qbd.astype(F32) * kn_ref[...].astype(F32), axis=1, keepdims=True)
        m_fin = jnp.maximum(m_ref[...], s_new)
        a_fin = jnp.exp(m_ref[...] - m_fin)
        p_new = jnp.exp(s_new - m_fin)
        num = a_fin * acc_ref[...] + p_new.astype(BF16).astype(F32) * vn_ref[...].astype(F32)
        o8 = num / (a_fin * l_ref[...] + p_new)
        own = (lax.broadcasted_iota(jnp.int32, o8.shape, 1) // HEAD_DIM
               == lax.broadcasted_iota(jnp.int32, o8.shape, 0))
        o_ref[...] = jnp.sum(jnp.where(own, o8, 0.0), axis=0, keepdims=True)


def _decode_attention(page_table, qbd, kn, vn, cn_bc, cache_kt, cache_vt, lf_sums):
    dec_b, n_pages = page_table.shape
    n = DECODE_PAGES
    n_steps = n_pages // n
    att_w = N_HEADS * HEAD_DIM

    def per_b(shape):
        return pl.BlockSpec((None,) + shape, lambda b, j, pt: (b,) + (0,) * len(shape))

    def paged(shape, i):
        return pl.BlockSpec((None,) + shape,
                            lambda b, j, pt: (pt[b, (n_steps - 1 - j) * n + i],) + (0,) * len(shape))

    kv_specs = [paged((N_HEADS, HEAD_DIM, CHUNK), i) for i in range(n)]
    lf_specs = [paged((N_HEADS, 2 * CHUNK), i) for i in range(n)]
    return pl.pallas_call(
        _decode_kernel,
        grid_spec=pltpu.PrefetchScalarGridSpec(
            num_scalar_prefetch=1,
            grid=(dec_b, n_steps),
            in_specs=[per_b((N_HEADS, att_w)), per_b((1, att_w)), per_b((1, att_w)), per_b((N_HEADS, CHUNK))]
                     + kv_specs + kv_specs + lf_specs,
            out_specs=per_b((1, att_w)),
            scratch_shapes=[pltpu.VMEM((N_HEADS, 1), F32), pltpu.VMEM((N_HEADS, 1), F32),
                            pltpu.VMEM((N_HEADS, att_w), F32), pltpu.VMEM((N_HEADS, CHUNK), F32)]),
        out_shape=jax.ShapeDtypeStruct((dec_b, 1, att_w), F32),
        compiler_params=_cparams(("arbitrary", "arbitrary")),
        name="fox_decode_attention",
    )(page_table, qbd, kn, vn, cn_bc, *([cache_kt] * n), *([cache_vt] * n), *([lf_sums] * n))


def _sample_conv_kernel(s0_ref, s1_ref, s2_ref, x_ref, sa_ref, cw_ref, cb_ref, alog_row_ref, u_ref, dec_ref):
    w = cw_ref[...]
    acc = cb_ref[...] + s0_ref[...] * w[0:1]
    acc = acc + s1_ref[...] * w[1:2]
    acc = acc + s2_ref[...] * w[2:3]
    acc = acc + x_ref[...] * w[3:4]
    u_ref[...] = _silu(acc)
    dec_ref[...] = jnp.exp(sa_ref[...] * (-jnp.exp(alog_row_ref[...])))


def _sample_conv(s0, s1, s2, xbc, small_act, conv_w, conv_b, alog_row):
    return pl.pallas_call(
        _sample_conv_kernel,
        out_shape=[jax.ShapeDtypeStruct(xbc.shape, F32), jax.ShapeDtypeStruct(small_act.shape, F32)],
        name="sample_conv",
    )(s0, s1, s2, xbc, small_act, conv_w, conv_b, alog_row)


SSD_STEP_ROWS = 8


def _ssd_step_kernel(dt_ref, dec_ref, dsk_ref, h_ref, xcol_ref, bc_ref, hn_ref, ycol_ref):
    gn = (N_HEADS // HEADS_PER_GROUP) * D_STATE
    for r in range(SSD_STEP_ROWS):
        b = pl.program_id(0) * SSD_STEP_ROWS + r
        for h in range(N_HEADS):
            g = h // HEADS_PER_GROUP
            xc = xcol_ref[r, h * HEAD_DIM:(h + 1) * HEAD_DIM, :]
            b_row = bc_ref[r, :, g * D_STATE:(g + 1) * D_STATE]
            c_row = bc_ref[r, :, gn + g * D_STATE:gn + (g + 1) * D_STATE]
            hn = dec_ref[b, h] * h_ref[r, h] + (dt_ref[b, h] * xc) * b_row
            hn_ref[r, h] = hn
            ycol_ref[r, h * HEAD_DIM:(h + 1) * HEAD_DIM, :] = (
                jnp.sum(hn * c_row, axis=1, keepdims=True) + dsk_ref[0, h] * xc)


def _ssd_step(dt, dec, d_skip, state, xcol, bc):
    dec_b = state.shape[0]
    rows = SSD_STEP_ROWS
    smem = pl.BlockSpec(memory_space=pltpu.SMEM)
    st_spec = pl.BlockSpec((rows,) + state.shape[1:], lambda b: (b, 0, 0, 0))
    col_spec = pl.BlockSpec((rows,) + xcol.shape[1:], lambda b: (b, 0, 0))
    return pl.pallas_call(
        _ssd_step_kernel,
        grid=(dec_b // rows,),
        in_specs=[smem, smem, smem, st_spec, col_spec,
                  pl.BlockSpec((rows,) + bc.shape[1:], lambda b: (b, 0, 0))],
        out_specs=[st_spec, col_spec],
        out_shape=[jax.ShapeDtypeStruct(state.shape, F32), jax.ShapeDtypeStruct(xcol.shape, F32)],
        compiler_params=_cparams(("arbitrary",)),
        name="ssd_step",
    )(dt, dec, d_skip, state, xcol, bc)


def kernel(x_prompt, x_sample, c_prompt, c_sample, cache_k, cache_v, cache_logf, state_conv, state_ssm, page_table, w_ada, b_ada, g_mix, w_in, b_f, conv_w, conv_b, dt_bias, A_log, D_skip, g_att_out, g_ssd_out, w_out, g_ffn, w_gate, w_up, w_down, g_final):
    batch, seq, d = x_prompt.shape
    dec_b = x_sample.shape[0]
    att_w = N_HEADS * HEAD_DIM
    ssd_w = g_ssd_out.shape[1]
    conv_dim = conv_w.shape[2]
    n_pool = cache_k.shape[1]
    l = 0

    wi = w_in[l]
    o_f = 3 * att_w
    o_z = o_f + N_HEADS
    o_x = o_z + ssd_w
    o_dt = o_x + conv_dim
    w_big = jnp.concatenate([wi[:, :o_f], wi[:, o_z:o_dt]], axis=1).astype(BF16)
    pad = jnp.zeros((d, LANES - 2 * N_HEADS), F32)
    w_small = jnp.concatenate([wi[:, o_f:o_z], wi[:, o_dt:], pad], axis=1).astype(BF16)
    zpad = jnp.zeros((LANES - 2 * N_HEADS,), F32)
    bias_small = jnp.concatenate([b_f[l], dt_bias[l], zpad])[None, :]
    alog_row = jnp.concatenate([jnp.zeros((N_HEADS,), F32), A_log[l], zpad])[None, :]
    alog_col = A_log[l][:, None]
    d_exp = jnp.repeat(D_skip[l], HEAD_DIM)[None, :]
    w_out_b = w_out[l].astype(BF16)
    wg_b, wu_b, wd_b = w_gate[l].astype(BF16), w_up[l].astype(BF16), w_down[l].astype(BF16)
    g_final2 = g_final[None, :]

    ada_rows = 144
    c_all = jnp.concatenate([c_prompt, c_sample, jnp.zeros((ada_rows - batch - dec_b, d), F32)], axis=0)
    mod = _ada(c_all, w_ada[l], b_ada[l][None, :])
    mod_p = mod[:batch].reshape(batch, 6, 1, d)
    sh1_p, sc1_p, ga1_p, sh2_p, sc2_p, ga2_p = (mod_p[:, i] for i in range(6))
    mod_s = mod[batch:batch + dec_b].reshape(dec_b, 6, d)
    sh1_s, sc1_s, ga1_s, sh2_s, sc2_s, ga2_s = (mod_s[:, i] for i in range(6))

    xp = x_prompt.reshape(batch * seq, d)
    tm_p = 512
    _, k_p, v_p, kb, _, z_p, xbc_p, sa_p, qt, vt = _inproj(
        xp, sh1_p, sc1_p, g_mix[l][None, :], w_big, w_small, bias_small, tm=tm_p, rows_per_mod=seq)
    logf_p = sa_p[:, :N_HEADS].reshape(batch, seq, N_HEADS)
    neg_f = _neg_cumsum(jnp.swapaxes(logf_p, 1, 2).reshape(batch * N_HEADS, seq))
    o_att_p = _prompt_attention(qt, kb, vt, neg_f.reshape(batch * N_HEADS // 2, 2, seq, 1), batch=batch, seq=seq)
    y_ssd_p, ht_p = _ssd_prompt(xbc_p, sa_p, conv_w[l], conv_b[l][None, :], alog_row, alog_col, d_exp,
                                batch=batch, seq=seq)
    x1_p = _merge_outproj(o_att_p, y_ssd_p, z_p, xp, ga1_p, g_att_out[l][None, :], g_ssd_out[l][None, :],
                          w_out_b, tm=tm_p, rows_per_mod=seq)
    y_p = _ffn_final(x1_p, sh2_p, sc2_p, ga2_p, g_ffn[l][None, :], wg_b, wu_b, wd_b, g_final2,
                     tm=tm_p, rows_per_mod=seq)

    xs = x_sample.reshape(dec_b, d)
    qs, k_s, v_s, ks_b, vs_b, z_s, xbc_s, sa_s, _, _ = _inproj(
        xs, sh1_s, sc1_s, g_mix[l][None, :], w_big, w_small, bias_small, tm=dec_b, rows_per_mod=1)
    lf_t = jnp.swapaxes(cache_logf[l], 1, 2).reshape(n_pool * N_HEADS, CHUNK)
    cache_kt = jnp.transpose(cache_k[l], (0, 2, 3, 1))
    cache_vt = jnp.transpose(cache_v[l], (0, 2, 3, 1))
    lf_sums = _page_suffix(lf_t)
    cn_bc = jnp.broadcast_to(sa_s[:, :N_HEADS, None], (dec_b, N_HEADS, CHUNK))
    own_head = jnp.eye(N_HEADS, dtype=bool)[None, :, :, None]
    qbd = jnp.where(own_head, qs.reshape(dec_b, 1, N_HEADS, HEAD_DIM), jnp.zeros((), BF16))
    o_att_s = _decode_attention(
        page_table, qbd.reshape(dec_b, N_HEADS, att_w), ks_b[:, None, :], vs_b[:, None, :], cn_bc,
        cache_kt, cache_vt,
        lf_sums.reshape(n_pool, N_HEADS, 2 * CHUNK)).reshape(dec_b, att_w)
    sc_l = state_conv[l]
    u_s, dec_s = _sample_conv(sc_l[:, 0], sc_l[:, 1], sc_l[:, 2], xbc_s, sa_s, conv_w[l], conv_b[l][None, :], alog_row)
    dt_s = sa_s[:, SMALL_DT0:SMALL_DT0 + N_HEADS]
    h_new, ycol = _ssd_step(dt_s, dec_s[:, SMALL_DT0:SMALL_DT0 + N_HEADS], D_skip[l][None, :], state_ssm[l],
                            u_s[:, :ssd_w].reshape(dec_b, ssd_w, 1), u_s[:, ssd_w:].reshape(dec_b, 1, conv_dim - ssd_w))
    x1_s = _merge_outproj(o_att_s, ycol.reshape(dec_b, ssd_w), z_s, xs, ga1_s, g_att_out[l][None, :],
                          g_ssd_out[l][None, :], w_out_b, tm=dec_b, rows_per_mod=1)
    y_s = _ffn_final(x1_s, sh2_s, sc2_s, ga2_s, g_ffn[l][None, :], wg_b, wu_b, wd_b, g_final2,
                     tm=dec_b, rows_per_mod=1)

    ht = ht_p.reshape(batch, N_HEADS // 2, D_STATE, 2, HEAD_DIM)
    ssm_prompt = jnp.transpose(ht, (0, 1, 3, 4, 2)).reshape(1, batch, N_HEADS, HEAD_DIM, D_STATE)
    conv_prompt = xbc_p.reshape(batch, seq, conv_dim)[:, seq - 3:, :][None]
    conv_sample = jnp.concatenate([sc_l[:, 1:], xbc_s[:, None, :]], axis=1)[None]
    return (y_p.reshape(batch, seq, d), y_s.reshape(dec_b, 1, d),
            k_p.reshape(1, batch, seq, N_HEADS, HEAD_DIM), v_p.reshape(1, batch, seq, N_HEADS, HEAD_DIM),
            logf_p[None],
            conv_prompt, ssm_prompt,
            k_s.reshape(1, dec_b, 1, N_HEADS, HEAD_DIM), v_s.reshape(1, dec_b, 1, N_HEADS, HEAD_DIM),
            sa_s[:, :N_HEADS].reshape(1, dec_b, 1, N_HEADS),
            conv_sample, h_new[None])
```

```python
import functools

import jax
import jax.numpy as jnp
from jax import lax
from jax.experimental import pallas as pl
from jax.experimental.pallas import tpu as pltpu

F32 = jnp.float32
BF16 = jnp.bfloat16

HEAD_DIM = 64
N_HEADS = 8
HEADS_PER_GROUP = 4
D_STATE = 128
CHUNK = 128
NORM_EPS = 1e-6
ATT_SCALE = HEAD_DIM ** -0.5
LANES = 128
SMALL_DT0 = 8
NEG_BIG = -1e30
VMEM_LIMIT = 56 * 1024 * 1024

NT_DIMS = (((1,), (1,)), ((), ()))


def _cparams(sem):
    return pltpu.CompilerParams(dimension_semantics=sem, vmem_limit_bytes=VMEM_LIMIT)


def _silu(x):
    return x * jax.nn.sigmoid(x)


def _softplus(x):
    return jnp.maximum(x, 0.0) + jnp.log1p(jnp.exp(-jnp.abs(x)))


def _rms(x, g):
    return x * lax.rsqrt(jnp.mean(x * x, axis=-1, keepdims=True) + NORM_EPS) * g


def _split3(x):
    hi = x.astype(BF16)
    r1 = x - hi.astype(F32)
    mid = r1.astype(BF16)
    lo = (r1 - mid.astype(F32)).astype(BF16)
    return hi, mid, lo


def _dot(a, b):
    return jnp.dot(a, b, preferred_element_type=F32)


def _dot3_right(x, m):
    hi, mid, lo = _split3(x)
    return _dot(hi, m) + _dot(mid, m) + _dot(lo, m)


def _dot3_left(m, x):
    hi, mid, lo = _split3(x)
    return _dot(m, hi) + _dot(m, mid) + _dot(m, lo)


def _tri(shape, cmp):
    r = lax.broadcasted_iota(jnp.int32, shape, 0)
    c = lax.broadcasted_iota(jnp.int32, shape, 1)
    return cmp(r, c)


def _ada_kernel(c_ref, w_ref, b_ref, o_ref):
    s = _silu(c_ref[...]).astype(BF16)
    o_ref[...] = _dot(s, w_ref[...].astype(BF16)) + b_ref[...]


def _ada(c_all, w_ada, b_ada):
    rows, d = c_all.shape
    n = w_ada.shape[1]
    tn = 512
    return pl.pallas_call(
        _ada_kernel,
        grid=(n // tn,),
        in_specs=[pl.BlockSpec((rows, d), lambda j: (0, 0)),
                  pl.BlockSpec((d, tn), lambda j: (0, j)),
                  pl.BlockSpec((1, tn), lambda j: (0, j))],
        out_specs=pl.BlockSpec((rows, tn), lambda j: (0, j)),
        out_shape=jax.ShapeDtypeStruct((rows, n), F32),
        compiler_params=_cparams(("arbitrary",)),
        name="ada_terms",
    )(c_all, w_ada, b_ada)


def _inproj_kernel(x_ref, sh_ref, sc_ref, g_ref, wb_ref, ws_ref, bias_ref,
                   q_ref, k_ref, v_ref, kb_ref, vb_ref, z_ref, xbc_ref, small_ref, qt_ref, vt_ref, *, att_w):
    h = (_rms(x_ref[...], g_ref[...]) * (1.0 + sc_ref[...]) + sh_ref[...]).astype(BF16)
    a = att_w
    q = _dot(h, wb_ref[:, 0:a]) * ATT_SCALE
    q_ref[...] = q.astype(BF16)
    qt_ref[...] = q.T.astype(BF16)
    k = _dot(h, wb_ref[:, a:2 * a])
    k_ref[...] = k
    kb_ref[...] = k.astype(BF16)
    v = _dot(h, wb_ref[:, 2 * a:3 * a])
    v_ref[...] = v
    vb_ref[...] = v.astype(BF16)
    vt_ref[...] = v.T.astype(BF16)
    z_ref[...] = _dot(h, wb_ref[:, 3 * a:4 * a])
    xbc_ref[...] = _dot(h, wb_ref[:, 4 * a:])
    s = _dot(h, ws_ref[...]) + bias_ref[...]
    lane = lax.broadcasted_iota(jnp.int32, s.shape, 1)
    small_ref[...] = jnp.where(lane < SMALL_DT0, -_softplus(-s), _softplus(s))


def _inproj(x, sh, sc, g_mix, w_big, w_small, bias_small, *, tm, rows_per_mod):
    rows, d = x.shape
    att_w = N_HEADS * HEAD_DIM
    conv_dim = w_big.shape[1] - 4 * att_w
    if rows_per_mod == 1:
        mod_spec = pl.BlockSpec((tm, d), lambda i: (i, 0))
    else:
        blocks_per_seq = rows_per_mod // tm
        mod_spec = pl.BlockSpec((None, 1, d), lambda i: (i // blocks_per_seq, 0, 0))
    row = lambda w: pl.BlockSpec((tm, w), lambda i: (i, 0))
    full = lambda a: pl.BlockSpec(a.shape, lambda i: (0,) * a.ndim)
    sds = lambda w, dt: jax.ShapeDtypeStruct((rows, w), dt)
    return pl.pallas_call(
        functools.partial(_inproj_kernel, att_w=att_w),
        grid=(rows // tm,),
        in_specs=[row(d), mod_spec, mod_spec, full(g_mix), full(w_big), full(w_small), full(bias_small)],
        out_specs=[row(att_w)] * 6 + [row(conv_dim), row(LANES)] + [pl.BlockSpec((att_w, tm), lambda i: (0, i))] * 2,
        out_shape=[sds(att_w, BF16), sds(att_w, F32), sds(att_w, F32), sds(att_w, BF16), sds(att_w, BF16),
                   sds(att_w, F32), sds(conv_dim, F32), sds(LANES, F32)]
                  + [jax.ShapeDtypeStruct((att_w, rows), BF16)] * 2,
        compiler_params=_cparams(("arbitrary",)),
        name="in_proj",
    )(x, sh, sc, g_mix, w_big, w_small, bias_small)


def _cumsum_kernel(x_ref, o_ref):
    n_blocks = x_ref.shape[1] // LANES
    upper = _tri((LANES, LANES), lambda r, c: r <= c).astype(BF16)
    carry = jnp.zeros((x_ref.shape[0], 1), F32)
    for c in range(n_blocks):
        cs = _dot3_right(x_ref[:, c * LANES:(c + 1) * LANES], upper) + carry
        o_ref[:, c * LANES:(c + 1) * LANES] = -cs
        carry = cs[:, LANES - 1:LANES]


def _neg_cumsum(x):
    return pl.pallas_call(
        _cumsum_kernel,
        out_shape=jax.ShapeDtypeStruct(x.shape, F32),
        name="logf_cumsum",
    )(x)


ATTN_TK = 512
ATTN_QS = 128


def _attn_kernel(qt_ref, k_ref, vt_ref, nf_ref, o_ref, *, tq):
    qi = pl.program_id(2)
    qt = qt_ref[...]
    row_q = lax.broadcasted_iota(jnp.int32, qt.shape, 0)
    qt_heads = (jnp.where(row_q < HEAD_DIM, qt, jnp.zeros_like(qt)),
                jnp.where(row_q >= HEAD_DIM, qt, jnp.zeros_like(qt)))

    tk = ATTN_TK
    qs = ATTN_QS
    n_strips = tq // qs

    def step(off, carry, diag):
        kblk = k_ref[pl.ds(off, tk), :]
        vtblk = vt_ref[:, pl.ds(off, tk)]
        out = []
        for hh in range(2):
            bias = jnp.broadcast_to(nf_ref[hh:hh + 1, pl.ds(off, tk)], (qs, tk)).T
            ms, ls, accs = carry[hh]
            new_m, new_l, new_acc = [], [], []
            for c in range(n_strips):
                m, l, acc = ms[c], ls[c], accs[c]
                first_q = c * qs
                if diag is not None and diag * tk > first_q + qs - 1:
                    new_m.append(m), new_l.append(l), new_acc.append(acc)
                    continue
                s = _dot(kblk, qt_heads[hh][:, first_q:first_q + qs]) + bias
                if diag is not None and diag * tk + tk - 1 > first_q:
                    key = lax.broadcasted_iota(jnp.int32, s.shape, 0) + diag * tk
                    qry = lax.broadcasted_iota(jnp.int32, s.shape, 1) + first_q
                    s = jnp.where(key <= qry, s, -jnp.inf)
                m_new = jnp.maximum(m, jnp.max(s, axis=0, keepdims=True))
                alpha = jnp.exp(m - m_new)
                p = jnp.exp(s - m_new)
                new_m.append(m_new)
                new_l.append(alpha * l + jnp.sum(p, axis=0, keepdims=True))
                new_acc.append(alpha * acc + _dot(vtblk, p.astype(BF16)))
            out.append((tuple(new_m), tuple(new_l), tuple(new_acc)))
        return tuple(out)

    init = tuple((tuple(jnp.full((1, qs), NEG_BIG, F32) for _ in range(n_strips)),
                  tuple(jnp.zeros((1, qs), F32) for _ in range(n_strips)),
                  tuple(jnp.zeros((LANES, qs), F32) for _ in range(n_strips))) for _ in range(2))
    carry = lax.fori_loop(0, qi * (tq // tk), lambda ki, c: step(pl.multiple_of(ki * tk, tk), c, None), init)
    for dblk in range(tq // tk):
        carry = step(pl.multiple_of(qi * tq + dblk * tk, tk), carry, dblk)
    (_, l_a, acc_a), (_, l_b, acc_b) = carry
    row_o = lax.broadcasted_iota(jnp.int32, (LANES, qs), 0)
    for c in range(n_strips):
        o_t = jnp.where(row_o < HEAD_DIM, acc_a[c] / l_a[c], acc_b[c] / l_b[c])
        o_ref[c * qs:(c + 1) * qs, :] = o_t.T


def _prompt_attention(qt, kb, vt, neg_f, *, batch, seq):
    tq = 512
    nq = seq // tq
    n_pairs = N_HEADS // 2
    return pl.pallas_call(
        functools.partial(_attn_kernel, tq=tq),
        grid=(batch, n_pairs, nq),
        in_specs=[pl.BlockSpec((LANES, tq), lambda b, p, i: (p, b * nq + i)),
                  pl.BlockSpec((seq, LANES), lambda b, p, i: (b, p)),
                  pl.BlockSpec((LANES, seq), lambda b, p, i: (p, b)),
                  pl.BlockSpec((None, 2, seq), lambda b, p, i: (b * n_pairs + p, 0, 0))],
        out_specs=pl.BlockSpec((tq, LANES), lambda b, p, i: (b * nq + i, p)),
        out_shape=jax.ShapeDtypeStruct(kb.shape, F32),
        compiler_params=_cparams(("arbitrary", "arbitrary", "arbitrary")),
        name="fox_prompt_attention",
    )(qt, kb, vt, neg_f)


def _ssd_prompt_kernel(xbc_ref, sa_ref, cw_ref, cb_ref, alog_row_ref, alog_col_ref, dexp_ref,
                       y_ref, ht_ref, ext_ref, hts_ref, *, ssd_w):
    c = pl.program_id(1)
    q_len = CHUNK
    tail = 8

    @pl.when(c == 0)
    def _():
        ext_ref[0:tail, :] = jnp.zeros((tail, ext_ref.shape[1]), F32)
        hts_ref[...] = jnp.zeros(hts_ref.shape, F32)

    x = xbc_ref[...]
    ext_ref[tail:tail + q_len, :] = x
    w = cw_ref[...]
    acc = cb_ref[...] + ext_ref[tail - 3:tail - 3 + q_len, :] * w[0:1]
    acc = acc + ext_ref[tail - 2:tail - 2 + q_len, :] * w[1:2]
    acc = acc + ext_ref[tail - 1:tail - 1 + q_len, :] * w[2:3]
    acc = acc + x * w[3:4]
    u = _silu(acc)
    ext_ref[0:tail, :] = x[q_len - tail:q_len, :]

    sa = sa_ref[...]
    lane = lax.broadcasted_iota(jnp.int32, (1, LANES), 1)
    dt_lanes = (lane >= SMALL_DT0) & (lane < SMALL_DT0 + N_HEADS)
    a_row = jnp.where(dt_lanes, -jnp.exp(alog_row_ref[...]), 0.0)
    lower = _tri((q_len, q_len), lambda r, c_: r >= c_)
    cs_full = _dot3_left(lower.astype(BF16), sa * a_row)
    dt_t = sa.T[SMALL_DT0:SMALL_DT0 + N_HEADS, :]
    da_t = dt_t * (-jnp.exp(alog_col_ref[...]))
    cs_t = _dot3_right(da_t, _tri((q_len, q_len), lambda r, c_: r <= c_).astype(BF16))
    tot_full = jnp.broadcast_to(cs_full[q_len - 1:q_len, :], cs_full.shape)

    lane_p = lax.broadcasted_iota(jnp.int32, (q_len, LANES), 1)
    first = lane_p < HEAD_DIM
    n_groups = N_HEADS // HEADS_PER_GROUP
    gn = n_groups * D_STATE
    for g in range(n_groups):
        bg = u[:, ssd_w + g * D_STATE: ssd_w + (g + 1) * D_STATE]
        cg = u[:, ssd_w + gn + g * D_STATE: ssd_w + gn + (g + 1) * D_STATE].astype(BF16)
        cb = lax.dot_general(cg, bg.astype(BF16), NT_DIMS, preferred_element_type=F32)
        bg_t = bg.T.astype(BF16)
        for k in range(HEADS_PER_GROUP // 2):
            pr = g * (HEADS_PER_GROUP // 2) + k
            x_pair = u[:, pr * LANES:(pr + 1) * LANES]
            x_pair_b = x_pair.astype(BF16)
            ys, cols, tots, dts = [], [], [], []
            for h in (2 * pr, 2 * pr + 1):
                col = cs_full[:, SMALL_DT0 + h:SMALL_DT0 + h + 1]
                seg = col - cs_t[h:h + 1, :]
                lmat = jnp.exp(jnp.where(lower, seg, -jnp.inf))
                mh = (cb * lmat * dt_t[h:h + 1, :]).astype(BF16)
                ys.append(_dot(mh, x_pair_b))
                cols.append(col)
                tots.append(tot_full[:, SMALL_DT0 + h:SMALL_DT0 + h + 1])
                dts.append(sa[:, SMALL_DT0 + h:SMALL_DT0 + h + 1])
            y_diag = jnp.where(first, ys[0], ys[1])
            h_prev = hts_ref[pr]
            e_col = jnp.where(first, jnp.exp(cols[0]), jnp.exp(cols[1]))
            y_off = _dot(cg, h_prev.astype(BF16)) * e_col
            y_ref[:, pr * LANES:(pr + 1) * LANES] = (
                y_diag + y_off + dexp_ref[:, pr * LANES:(pr + 1) * LANES] * x_pair)
            w_pair = jnp.where(first, jnp.exp(tots[0] - cols[0]) * dts[0], jnp.exp(tots[1] - cols[1]) * dts[1])
            new_t = _dot(bg_t, (x_pair * w_pair).astype(BF16))
            dec = jnp.where(first, jnp.exp(tots[0]), jnp.exp(tots[1]))
            hts_ref[pr] = h_prev * dec + new_t

    @pl.when(c == pl.num_programs(1) - 1)
    def _():
        ht_ref[...] = hts_ref[...]


def _ssd_prompt(xbc, small_act, conv_w, conv_b, alog_row, alog_col, d_exp, *, batch, seq):
    ssd_w = d_exp.shape[1]
    conv_dim = xbc.shape[1]
    nc = seq // CHUNK
    n_pairs = N_HEADS // 2
    full = lambda a: pl.BlockSpec(a.shape, lambda b, c: (0,) * a.ndim)
    return pl.pallas_call(
        functools.partial(_ssd_prompt_kernel, ssd_w=ssd_w),
        grid=(batch, nc),
        in_specs=[pl.BlockSpec((CHUNK, conv_dim), lambda b, c: (b * nc + c, 0)),
                  pl.BlockSpec((CHUNK, LANES), lambda b, c: (b * nc + c, 0)),
                  full(conv_w), full(conv_b), full(alog_row), full(alog_col), full(d_exp)],
        out_specs=[pl.BlockSpec((CHUNK, ssd_w), lambda b, c: (b * nc + c, 0)),
                   pl.BlockSpec((None, n_pairs, D_STATE, LANES), lambda b, c: (b, 0, 0, 0))],
        out_shape=[jax.ShapeDtypeStruct((batch * seq, ssd_w), F32),
                   jax.ShapeDtypeStruct((batch, n_pairs, D_STATE, LANES), F32)],
        scratch_shapes=[pltpu.VMEM((CHUNK + 8, conv_dim), F32),
                        pltpu.VMEM((n_pairs, D_STATE, LANES), F32)],
        compiler_params=_cparams(("arbitrary", "arbitrary")),
        name="ssd_prompt",
    )(xbc, small_act, conv_w, conv_b, alog_row, alog_col, d_exp)


def _merge_kernel(o_ref, y_ref, z_ref, x_ref, ga_ref, gatt_ref, gssd_ref, w_ref, out_ref, *, att_w):
    att = _rms(o_ref[...], gatt_ref[...]).astype(BF16)
    ssd = _rms(y_ref[...] * _silu(z_ref[...]), gssd_ref[...]).astype(BF16)
    mix = _dot(att, w_ref[0:att_w, :]) + _dot(ssd, w_ref[att_w:, :])
    out_ref[...] = x_ref[...] + ga_ref[...] * mix


def _mod_spec(tm, d, rows_per_mod):
    if rows_per_mod == 1:
        return pl.BlockSpec((tm, d), lambda i, *_: (i, 0))
    blocks_per_seq = rows_per_mod // tm
    return pl.BlockSpec((None, 1, d), lambda i, *_: (i // blocks_per_seq, 0, 0))


def _merge_outproj(o_att, y_ssd, z, x, ga1, g_att, g_ssd, w_out, *, tm, rows_per_mod):
    rows, d = x.shape
    att_w = o_att.shape[1]
    row = lambda w: pl.BlockSpec((tm, w), lambda i: (i, 0))
    full = lambda a: pl.BlockSpec(a.shape, lambda i: (0,) * a.ndim)
    return pl.pallas_call(
        functools.partial(_merge_kernel, att_w=att_w),
        grid=(rows // tm,),
        in_specs=[row(att_w), row(att_w), row(att_w), row(d), _mod_spec(tm, d, rows_per_mod),
                  full(g_att), full(g_ssd), full(w_out)],
        out_specs=row(d),
        out_shape=jax.ShapeDtypeStruct((rows, d), F32),
        compiler_params=_cparams(("arbitrary",)),
        name="merge_out_proj",
    )(o_att, y_ssd, z, x, ga1, g_att, g_ssd, w_out)


def _ffn_kernel(x_ref, sh_ref, sc_ref, ga_ref, g_ref, wg_ref, wu_ref, wd_ref, gf_ref, out_ref, h_ref, acc_ref):
    j = pl.program_id(1)

    @pl.when(j == 0)
    def _():
        h_ref[...] = (_rms(x_ref[...], g_ref[...]) * (1.0 + sc_ref[...]) + sh_ref[...]).astype(BF16)
        acc_ref[...] = jnp.zeros(acc_ref.shape, F32)

    h = h_ref[...]
    a = (_silu(_dot(h, wg_ref[...])) * _dot(h, wu_ref[...])).astype(BF16)
    acc_ref[...] += _dot(a, wd_ref[...])

    @pl.when(j == pl.num_programs(1) - 1)
    def _():
        out_ref[...] = _rms(x_ref[...] + ga_ref[...] * acc_ref[...], gf_ref[...])


def _ffn_final(x, sh, sc, ga, g_ffn, w_gate, w_up, w_down, g_final, *, tm, rows_per_mod):
    rows, d = x.shape
    hidden = w_gate.shape[1]
    th = hidden // 2
    mod = _mod_spec(tm, d, rows_per_mod)
    full = lambda a: pl.BlockSpec(a.shape, lambda i, j: (0,) * a.ndim)
    return pl.pallas_call(
        _ffn_kernel,
        grid=(rows // tm, hidden // th),
        in_specs=[pl.BlockSpec((tm, d), lambda i, j: (i, 0)), mod, mod, mod, full(g_ffn),
                  pl.BlockSpec((d, th), lambda i, j: (0, j)),
                  pl.BlockSpec((d, th), lambda i, j: (0, j)),
                  pl.BlockSpec((th, d), lambda i, j: (j, 0)),
                  full(g_final)],
        out_specs=pl.BlockSpec((tm, d), lambda i, j: (i, 0)),
        out_shape=jax.ShapeDtypeStruct((rows, d), F32),
        scratch_shapes=[pltpu.VMEM((tm, d), BF16), pltpu.VMEM((tm, d), F32)],
        compiler_params=_cparams(("arbitrary", "arbitrary")),
        name="ffn_final_norm",
    )(x, sh, sc, ga, g_ffn, w_gate, w_up, w_down, g_final)


def _page_suffix_kernel(x_ref, o_ref):
    x = x_ref[...]
    strict = _tri((LANES, LANES), lambda r, c: r > c).astype(BF16)
    o_ref[:, 0:LANES] = _dot3_right(x, strict)
    o_ref[:, LANES:2 * LANES] = _dot3_right(x, jnp.ones((LANES, LANES), BF16))


def _page_suffix(lf_t):
    rows = lf_t.shape[0]
    tr = 2048
    return pl.pallas_call(
        _page_suffix_kernel,
        grid=(rows // tr,),
        in_specs=[pl.BlockSpec((tr, LANES), lambda i: (i, 0))],
        out_specs=pl.BlockSpec((tr, 2 * LANES), lambda i: (i, 0)),
        out_shape=jax.ShapeDtypeStruct((rows, 2 * LANES), F32),
        compiler_params=_cparams(("arbitrary",)),
        name="page_logf_suffix",
    )(lf_t)


DECODE_PAGES = 16


def _decode_kernel(pt_ref, qbd_ref, kn_ref, vn_ref, cn_ref, *refs):
    n = DECODE_PAGES
    k_refs, v_refs, lf_refs = refs[0:n], refs[n:2 * n], refs[2 * n:3 * n]
    o_ref, m_ref, l_ref, acc_ref, carry_ref = refs[3 * n:]
    j = pl.program_id(1)
    att_w = N_HEADS * HEAD_DIM

    @pl.when(j == 0)
    def _():
        m_ref[...] = jnp.full(m_ref.shape, NEG_BIG, F32)
        l_ref[...] = jnp.zeros(l_ref.shape, F32)
        acc_ref[...] = jnp.zeros(acc_ref.shape, F32)
        carry_ref[...] = cn_ref[...]

    qbd = qbd_ref[...]
    carry = carry_ref[...]
    s_parts = [None] * n
    for i in reversed(range(n)):
        kt = k_refs[i][...].reshape(att_w, CHUNK).astype(BF16)
        s_parts[i] = _dot(qbd, kt) + (lf_refs[i][:, 0:CHUNK] + carry)
        carry = carry + lf_refs[i][:, CHUNK:2 * CHUNK]
    carry_ref[...] = carry
    s = jnp.concatenate(s_parts, axis=1)
    m_old = m_ref[...]
    m_new = jnp.maximum(m_old, jnp.max(s, axis=1, keepdims=True))
    alpha = jnp.exp(m_old - m_new)
    p = jnp.exp(s - m_new)
    l_ref[...] = alpha * l_ref[...] + jnp.sum(p, axis=1, keepdims=True)
    m_ref[...] = m_new
    pb = p.astype(BF16)
    pv = jnp.zeros(acc_ref.shape, F32)
    for i in range(n):
        vt = v_refs[i][...].reshape(att_w, CHUNK).astype(BF16)
        pv = pv + lax.dot_general(pb[:, i * CHUNK:(i + 1) * CHUNK], vt, NT_DIMS, preferred_element_type=F32)
    acc_ref[...] = alpha * acc_ref[...] + pv

    @pl.when(j == pl.num_programs(1) - 1)
    def _():
        s_new = jnp.sum(qbd.astype(F32) * kn_ref[...].astype(F32), axis=1, keepdims=True)
        m_fin = jnp.maximum(m_ref[...], s_new)
        a_fin = jnp.exp(m_ref[...] - m_fin)
        p_new = jnp.exp(s_new - m_fin)
        num = a_fin * acc_ref[...] + p_new.astype(BF16).astype(F32) * vn_ref[...].astype(F32)
        o8 = num / (a_fin * l_ref[...] + p_new)
        own = (lax.broadcasted_iota(jnp.int32, o8.shape, 1) // HEAD_DIM
               == lax.broadcasted_iota(jnp.int32, o8.shape, 0))
        o_ref[...] = jnp.sum(jnp.where(own, o8, 0.0), axis=0, keepdims=True)


def _decode_attention(page_table, qbd, kn, vn, cn_bc, cache_kt, cache_vt, lf_sums):
    dec_b, n_pages = page_table.shape
    n = DECODE_PAGES
    n_steps = n_pages // n
    att_w = N_HEADS * HEAD_DIM

    def per_b(shape):
        return pl.BlockSpec((None,) + shape, lambda b, j, pt: (b,) + (0,) * len(shape))

    def paged(shape, i):
        return pl.BlockSpec((None,) + shape,
                            lambda b, j, pt: (pt[b, (n_steps - 1 - j) * n + i],) + (0,) * len(shape))

    kv_specs = [paged((N_HEADS, HEAD_DIM, CHUNK), i) for i in range(n)]
    lf_specs = [paged((N_HEADS, 2 * CHUNK), i) for i in range(n)]
    return pl.pallas_call(
        _decode_kernel,
        grid_spec=pltpu.PrefetchScalarGridSpec(
            num_scalar_prefetch=1,
            grid=(dec_b, n_steps),
            in_specs=[per_b((N_HEADS, att_w)), per_b((1, att_w)), per_b((1, att_w)), per_b((N_HEADS, CHUNK))]
                     + kv_specs + kv_specs + lf_specs,
            out_specs=per_b((1, att_w)),
            scratch_shapes=[pltpu.VMEM((N_HEADS, 1), F32), pltpu.VMEM((N_HEADS, 1), F32),
                            pltpu.VMEM((N_HEADS, att_w), F32), pltpu.VMEM((N_HEADS, CHUNK), F32)]),
        out_shape=jax.ShapeDtypeStruct((dec_b, 1, att_w), F32),
        compiler_params=_cparams(("arbitrary", "arbitrary")),
        name="fox_decode_attention",
    )(page_table, qbd, kn, vn, cn_bc, *([cache_kt] * n), *([cache_vt] * n), *([lf_sums] * n))


def _sample_conv_kernel(s0_ref, s1_ref, s2_ref, x_ref, sa_ref, cw_ref, cb_ref, alog_row_ref, u_ref, dec_ref):
    w = cw_ref[...]
    acc = cb_ref[...] + s0_ref[...] * w[0:1]
    acc = acc + s1_ref[...] * w[1:2]
    acc = acc + s2_ref[...] * w[2:3]
    acc = acc + x_ref[...] * w[3:4]
    u_ref[...] = _silu(acc)
    dec_ref[...] = jnp.exp(sa_ref[...] * (-jnp.exp(alog_row_ref[...])))


def _sample_conv(s0, s1, s2, xbc, small_act, conv_w, conv_b, alog_row):
    return pl.pallas_call(
        _sample_conv_kernel,
        out_shape=[jax.ShapeDtypeStruct(xbc.shape, F32), jax.ShapeDtypeStruct(small_act.shape, F32)],
        name="sample_conv",
    )(s0, s1, s2, xbc, small_act, conv_w, conv_b, alog_row)


SSD_STEP_ROWS = 8


def _ssd_step_kernel(dt_ref, dec_ref, h_ref, xb_ref, b_ref, c_ref, dcol_ref, hn_ref, y_ref):
    n_groups = N_HEADS // HEADS_PER_GROUP
    for r in range(SSD_STEP_ROWS):
        b = pl.program_id(0) * SSD_STEP_ROWS + r
        xb = xb_ref[r]
        for h in range(N_HEADS):
            g = h // HEADS_PER_GROUP
            x_h = xb[h * HEAD_DIM:(h + 1) * HEAD_DIM, :]
            hn_ref[r, h] = dec_ref[b, h] * h_ref[r, h] + (dt_ref[b, h] * x_h) * b_ref[r, g:g + 1, :]
        hn_all = hn_ref[r].reshape(N_HEADS * HEAD_DIM, D_STATE).astype(BF16)
        yc = lax.dot_general(hn_all, c_ref[r].astype(BF16), NT_DIMS, preferred_element_type=F32)
        y_ref[r] = yc + dcol_ref[...] * xb[:, 0:yc.shape[1]]


def _ssd_step(dt, dec, state, xb, b_rows, c_rows, dcol):
    dec_b = state.shape[0]
    rows = SSD_STEP_ROWS
    smem = pl.BlockSpec(memory_space=pltpu.SMEM)
    blk = lambda a: pl.BlockSpec((rows,) + a.shape[1:], lambda b: (b,) + (0,) * (a.ndim - 1))
    y_shape = (dec_b, xb.shape[1], c_rows.shape[1])
    return pl.pallas_call(
        _ssd_step_kernel,
        grid=(dec_b // rows,),
        in_specs=[smem, smem, blk(state), blk(xb), blk(b_rows), blk(c_rows),
                  pl.BlockSpec(dcol.shape, lambda b: (0, 0))],
        out_specs=[blk(state), pl.BlockSpec((rows,) + y_shape[1:], lambda b: (b, 0, 0))],
        out_shape=[jax.ShapeDtypeStruct(state.shape, F32), jax.ShapeDtypeStruct(y_shape, F32)],
        compiler_params=_cparams(("arbitrary",)),
        name="ssd_step",
    )(dt, dec, state, xb, b_rows, c_rows, dcol)


def kernel(x_prompt, x_sample, c_prompt, c_sample, cache_k, cache_v, cache_logf, state_conv, state_ssm, page_table, w_ada, b_ada, g_mix, w_in, b_f, conv_w, conv_b, dt_bias, A_log, D_skip, g_att_out, g_ssd_out, w_out, g_ffn, w_gate, w_up, w_down, g_final):
    batch, seq, d = x_prompt.shape
    dec_b = x_sample.shape[0]
    att_w = N_HEADS * HEAD_DIM
    ssd_w = g_ssd_out.shape[1]
    conv_dim = conv_w.shape[2]
    n_pool = cache_k.shape[1]
    l = 0

    wi = w_in[l]
    o_f = 3 * att_w
    o_z = o_f + N_HEADS
    o_x = o_z + ssd_w
    o_dt = o_x + conv_dim
    w_big = jnp.concatenate([wi[:, :o_f], wi[:, o_z:o_dt]], axis=1).astype(BF16)
    pad = jnp.zeros((d, LANES - 2 * N_HEADS), F32)
    w_small = jnp.concatenate([wi[:, o_f:o_z], wi[:, o_dt:], pad], axis=1).astype(BF16)
    zpad = jnp.zeros((LANES - 2 * N_HEADS,), F32)
    bias_small = jnp.concatenate([b_f[l], dt_bias[l], zpad])[None, :]
    alog_row = jnp.concatenate([jnp.zeros((N_HEADS,), F32), A_log[l], zpad])[None, :]
    alog_col = A_log[l][:, None]
    d_exp = jnp.repeat(D_skip[l], HEAD_DIM)[None, :]
    w_out_b = w_out[l].astype(BF16)
    wg_b, wu_b, wd_b = w_gate[l].astype(BF16), w_up[l].astype(BF16), w_down[l].astype(BF16)
    g_final2 = g_final[None, :]

    ada_rows = 144
    c_all = jnp.concatenate([c_prompt, c_sample, jnp.zeros((ada_rows - batch - dec_b, d), F32)], axis=0)
    mod = _ada(c_all, w_ada[l], b_ada[l][None, :])
    mod_p = mod[:batch].reshape(batch, 6, 1, d)
    sh1_p, sc1_p, ga1_p, sh2_p, sc2_p, ga2_p = (mod_p[:, i] for i in range(6))
    mod_s = mod[batch:batch + dec_b].reshape(dec_b, 6, d)
    sh1_s, sc1_s, ga1_s, sh2_s, sc2_s, ga2_s = (mod_s[:, i] for i in range(6))

    xp = x_prompt.reshape(batch * seq, d)
    tm_p = 512
    _, k_p, v_p, kb, _, z_p, xbc_p, sa_p, qt, vt = _inproj(
        xp, sh1_p, sc1_p, g_mix[l][None, :], w_big, w_small, bias_small, tm=tm_p, rows_per_mod=seq)
    logf_p = sa_p[:, :N_HEADS].reshape(batch, seq, N_HEADS)
    neg_f = _neg_cumsum(jnp.swapaxes(logf_p, 1, 2).reshape(batch * N_HEADS, seq))
    o_att_p = _prompt_attention(qt, kb, vt, neg_f.reshape(batch * N_HEADS // 2, 2, seq), batch=batch, seq=seq)
    y_ssd_p, ht_p = _ssd_prompt(xbc_p, sa_p, conv_w[l], conv_b[l][None, :], alog_row, alog_col, d_exp,
                                batch=batch, seq=seq)
    x1_p = _merge_outproj(o_att_p, y_ssd_p, z_p, xp, ga1_p, g_att_out[l][None, :], g_ssd_out[l][None, :],
                          w_out_b, tm=tm_p, rows_per_mod=seq)
    y_p = _ffn_final(x1_p, sh2_p, sc2_p, ga2_p, g_ffn[l][None, :], wg_b, wu_b, wd_b, g_final2,
                     tm=tm_p, rows_per_mod=seq)

    xs = x_sample.reshape(dec_b, d)
    qs, k_s, v_s, ks_b, vs_b, z_s, xbc_s, sa_s, _, _ = _inproj(
        xs, sh1_s, sc1_s, g_mix[l][None, :], w_big, w_small, bias_small, tm=dec_b, rows_per_mod=1)
    lf_t = jnp.swapaxes(cache_logf[l], 1, 2).reshape(n_pool * N_HEADS, CHUNK)
    cache_kt = jnp.transpose(cache_k[l], (0, 2, 3, 1))
    cache_vt = jnp.transpose(cache_v[l], (0, 2, 3, 1))
    lf_sums = _page_suffix(lf_t)
    cn_bc = jnp.broadcast_to(sa_s[:, :N_HEADS, None], (dec_b, N_HEADS, CHUNK))
    own_head = jnp.eye(N_HEADS, dtype=bool)[None, :, :, None]
    qbd = jnp.where(own_head, qs.reshape(dec_b, 1, N_HEADS, HEAD_DIM), jnp.zeros((), BF16))
    o_att_s = _decode_attention(
        page_table, qbd.reshape(dec_b, N_HEADS, att_w), ks_b[:, None, :], vs_b[:, None, :], cn_bc,
        cache_kt, cache_vt,
        lf_sums.reshape(n_pool, N_HEADS, 2 * CHUNK)).reshape(dec_b, att_w)
    sc_l = state_conv[l]
    u_s, dec_s = _sample_conv(sc_l[:, 0], sc_l[:, 1], sc_l[:, 2], xbc_s, sa_s, conv_w[l], conv_b[l][None, :], alog_row)
    dt_s = sa_s[:, SMALL_DT0:SMALL_DT0 + N_HEADS]
    n_groups = N_HEADS // HEADS_PER_GROUP
    gn = n_groups * D_STATE
    xb = jnp.broadcast_to(u_s[:, :ssd_w, None], (dec_b, ssd_w, LANES))
    b_rows = u_s[:, ssd_w:ssd_w + gn].reshape(dec_b, n_groups, D_STATE)
    c_rows = jnp.pad(u_s[:, ssd_w + gn:].reshape(dec_b, n_groups, D_STATE), ((0, 0), (0, 8 - n_groups), (0, 0)))
    dcol = jnp.broadcast_to(d_exp[0][:, None], (ssd_w, 8))
    h_new, y8 = _ssd_step(dt_s, dec_s[:, SMALL_DT0:SMALL_DT0 + N_HEADS], state_ssm[l], xb, b_rows, c_rows, dcol)
    rows_per_group = ssd_w // n_groups
    y_ssd_s = jnp.concatenate([y8[:, g * rows_per_group:(g + 1) * rows_per_group, g] for g in range(n_groups)], axis=1)
    x1_s = _merge_outproj(o_att_s, y_ssd_s, z_s, xs, ga1_s, g_att_out[l][None, :],
                          g_ssd_out[l][None, :], w_out_b, tm=dec_b, rows_per_mod=1)
    y_s = _ffn_final(x1_s, sh2_s, sc2_s, ga2_s, g_ffn[l][None, :], wg_b, wu_b, wd_b, g_final2,
                     tm=dec_b, rows_per_mod=1)

    ht = ht_p.reshape(batch, N_HEADS // 2, D_STATE, 2, HEAD_DIM)
    ssm_prompt = jnp.transpose(ht, (0, 1, 3, 4, 2)).reshape(1, batch, N_HEADS, HEAD_DIM, D_STATE)
    conv_prompt = xbc_p.reshape(batch, seq, conv_dim)[:, seq - 3:, :][None]
    conv_sample = jnp.concatenate([sc_l[:, 1:], xbc_s[:, None, :]], axis=1)[None]
    return (y_p.reshape(batch, seq, d), y_s.reshape(dec_b, 1, d),
            k_p.reshape(1, batch, seq, N_HEADS, HEAD_DIM), v_p.reshape(1, batch, seq, N_HEADS, HEAD_DIM),
            logf_p[None],
            conv_prompt, ssm_prompt,
            k_s.reshape(1, dec_b, 1, N_HEADS, HEAD_DIM), v_s.reshape(1, dec_b, 1, N_HEADS, HEAD_DIM),
            sa_s[:, :N_HEADS].reshape(1, dec_b, 1, N_HEADS),
            conv_sample, h_new[None])
```

```python
import functools

import jax
import jax.numpy as jnp
from jax import lax
from jax.experimental import pallas as pl
from jax.experimental.pallas import tpu as pltpu

F32 = jnp.float32
BF16 = jnp.bfloat16

HEAD_DIM = 64
N_HEADS = 8
HEADS_PER_GROUP = 4
D_STATE = 128
CHUNK = 128
NORM_EPS = 1e-6
ATT_SCALE = HEAD_DIM ** -0.5
LANES = 128
SMALL_DT0 = 8
NEG_BIG = -1e30
VMEM_LIMIT = 56 * 1024 * 1024

NT_DIMS = (((1,), (1,)), ((), ()))


def _cparams(sem):
    return pltpu.CompilerParams(dimension_semantics=sem, vmem_limit_bytes=VMEM_LIMIT)


def _silu(x):
    return x * jax.nn.sigmoid(x)


def _softplus(x):
    return jnp.maximum(x, 0.0) + jnp.log1p(jnp.exp(-jnp.abs(x)))


def _rms(x, g):
    return x * lax.rsqrt(jnp.mean(x * x, axis=-1, keepdims=True) + NORM_EPS) * g


def _split3(x):
    hi = x.astype(BF16)
    r1 = x - hi.astype(F32)
    mid = r1.astype(BF16)
    lo = (r1 - mid.astype(F32)).astype(BF16)
    return hi, mid, lo


def _dot(a, b):
    return jnp.dot(a, b, preferred_element_type=F32)


def _dot3_right(x, m):
    hi, mid, lo = _split3(x)
    return _dot(hi, m) + _dot(mid, m) + _dot(lo, m)


def _dot3_left(m, x):
    hi, mid, lo = _split3(x)
    return _dot(m, hi) + _dot(m, mid) + _dot(m, lo)


def _tri(shape, cmp):
    r = lax.broadcasted_iota(jnp.int32, shape, 0)
    c = lax.broadcasted_iota(jnp.int32, shape, 1)
    return cmp(r, c)


def _ada_kernel(c_ref, w_ref, b_ref, o_ref):
    s = _silu(c_ref[...]).astype(BF16)
    o_ref[...] = _dot(s, w_ref[...].astype(BF16)) + b_ref[...]


def _ada(c_all, w_ada, b_ada):
    rows, d = c_all.shape
    n = w_ada.shape[1]
    tn = 512
    return pl.pallas_call(
        _ada_kernel,
        grid=(n // tn,),
        in_specs=[pl.BlockSpec((rows, d), lambda j: (0, 0)),
                  pl.BlockSpec((d, tn), lambda j: (0, j)),
                  pl.BlockSpec((1, tn), lambda j: (0, j))],
        out_specs=pl.BlockSpec((rows, tn), lambda j: (0, j)),
        out_shape=jax.ShapeDtypeStruct((rows, n), F32),
        compiler_params=_cparams(("arbitrary",)),
        name="ada_terms",
    )(c_all, w_ada, b_ada)


def _inproj_kernel(x_ref, sh_ref, sc_ref, g_ref, wb_ref, ws_ref, bias_ref,
                   q_ref, k_ref, v_ref, kb_ref, vb_ref, z_ref, xbc_ref, small_ref, qt_ref, vt_ref, *, att_w):
    h = (_rms(x_ref[...], g_ref[...]) * (1.0 + sc_ref[...]) + sh_ref[...]).astype(BF16)
    a = att_w
    q = _dot(h, wb_ref[:, 0:a]) * ATT_SCALE
    q_ref[...] = q.astype(BF16)
    qt_ref[...] = q.T.astype(BF16)
    k = _dot(h, wb_ref[:, a:2 * a])
    k_ref[...] = k
    kb_ref[...] = k.astype(BF16)
    v = _dot(h, wb_ref[:, 2 * a:3 * a])
    v_ref[...] = v
    vb_ref[...] = v.astype(BF16)
    vt_ref[...] = v.T.astype(BF16)
    z_ref[...] = _dot(h, wb_ref[:, 3 * a:4 * a])
    xbc_ref[...] = _dot(h, wb_ref[:, 4 * a:])
    s = _dot(h, ws_ref[...]) + bias_ref[...]
    lane = lax.broadcasted_iota(jnp.int32, s.shape, 1)
    small_ref[...] = jnp.where(lane < SMALL_DT0, -_softplus(-s), _softplus(s))


def _inproj(x, sh, sc, g_mix, w_big, w_small, bias_small, *, tm, rows_per_mod):
    rows, d = x.shape
    att_w = N_HEADS * HEAD_DIM
    conv_dim = w_big.shape[1] - 4 * att_w
    if rows_per_mod == 1:
        mod_spec = pl.BlockSpec((tm, d), lambda i: (i, 0))
    else:
        blocks_per_seq = rows_per_mod // tm
        mod_spec = pl.BlockSpec((None, 1, d), lambda i: (i // blocks_per_seq, 0, 0))
    row = lambda w: pl.BlockSpec((tm, w), lambda i: (i, 0))
    full = lambda a: pl.BlockSpec(a.shape, lambda i: (0,) * a.ndim)
    sds = lambda w, dt: jax.ShapeDtypeStruct((rows, w), dt)
    return pl.pallas_call(
        functools.partial(_inproj_kernel, att_w=att_w),
        grid=(rows // tm,),
        in_specs=[row(d), mod_spec, mod_spec, full(g_mix), full(w_big), full(w_small), full(bias_small)],
        out_specs=[row(att_w)] * 6 + [row(conv_dim), row(LANES)] + [pl.BlockSpec((att_w, tm), lambda i: (0, i))] * 2,
        out_shape=[sds(att_w, BF16), sds(att_w, F32), sds(att_w, F32), sds(att_w, BF16), sds(att_w, BF16),
                   sds(att_w, F32), sds(conv_dim, F32), sds(LANES, F32)]
                  + [jax.ShapeDtypeStruct((att_w, rows), BF16)] * 2,
        compiler_params=_cparams(("arbitrary",)),
        name="in_proj",
    )(x, sh, sc, g_mix, w_big, w_small, bias_small)


def _cumsum_kernel(x_ref, o_ref):
    n_blocks = x_ref.shape[1] // LANES
    upper = _tri((LANES, LANES), lambda r, c: r <= c).astype(BF16)
    carry = jnp.zeros((x_ref.shape[0], 1), F32)
    for c in range(n_blocks):
        cs = _dot3_right(x_ref[:, c * LANES:(c + 1) * LANES], upper) + carry
        o_ref[:, c * LANES:(c + 1) * LANES] = -cs
        carry = cs[:, LANES - 1:LANES]


def _neg_cumsum(x):
    return pl.pallas_call(
        _cumsum_kernel,
        out_shape=jax.ShapeDtypeStruct(x.shape, F32),
        name="logf_cumsum",
    )(x)


ATTN_TK = 512
ATTN_QS = 128


def _attn_kernel(qt_ref, k_ref, vt_ref, nf_ref, o_ref, *, tq):
    qi = pl.program_id(2)
    qt = qt_ref[...]
    row_q = lax.broadcasted_iota(jnp.int32, qt.shape, 0)
    qt_heads = (jnp.where(row_q < HEAD_DIM, qt, jnp.zeros_like(qt)),
                jnp.where(row_q >= HEAD_DIM, qt, jnp.zeros_like(qt)))

    tk = ATTN_TK
    qs = ATTN_QS
    n_strips = tq // qs

    def step(off, carry, diag):
        kblk = k_ref[pl.ds(off, tk), :]
        vtblk = vt_ref[:, pl.ds(off, tk)]
        out = []
        for hh in range(2):
            bias = jnp.broadcast_to(nf_ref[hh:hh + 1, pl.ds(off, tk)], (qs, tk)).T
            ms, ls, accs = carry[hh]
            new_m, new_l, new_acc = [], [], []
            for c in range(n_strips):
                m, l, acc = ms[c], ls[c], accs[c]
                first_q = c * qs
                if diag is not None and diag * tk > first_q + qs - 1:
                    new_m.append(m), new_l.append(l), new_acc.append(acc)
                    continue
                s = _dot(kblk, qt_heads[hh][:, first_q:first_q + qs]) + bias
                if diag is not None and diag * tk + tk - 1 > first_q:
                    key = lax.broadcasted_iota(jnp.int32, s.shape, 0) + diag * tk
                    qry = lax.broadcasted_iota(jnp.int32, s.shape, 1) + first_q
                    s = jnp.where(key <= qry, s, -jnp.inf)
                m_new = jnp.maximum(m, jnp.max(s, axis=0, keepdims=True))
                alpha = jnp.exp(m - m_new)
                p = jnp.exp(s - m_new)
                new_m.append(m_new)
                new_l.append(alpha * l + jnp.sum(p, axis=0, keepdims=True))
                new_acc.append(alpha * acc + _dot(vtblk, p.astype(BF16)))
            out.append((tuple(new_m), tuple(new_l), tuple(new_acc)))
        return tuple(out)

    init = tuple((tuple(jnp.full((1, qs), NEG_BIG, F32) for _ in range(n_strips)),
                  tuple(jnp.zeros((1, qs), F32) for _ in range(n_strips)),
                  tuple(jnp.zeros((LANES, qs), F32) for _ in range(n_strips))) for _ in range(2))
    carry = lax.fori_loop(0, qi * (tq // tk), lambda ki, c: step(pl.multiple_of(ki * tk, tk), c, None), init)
    for dblk in range(tq // tk):
        carry = step(pl.multiple_of(qi * tq + dblk * tk, tk), carry, dblk)
    (_, l_a, acc_a), (_, l_b, acc_b) = carry
    row_o = lax.broadcasted_iota(jnp.int32, (LANES, qs), 0)
    for c in range(n_strips):
        o_t = jnp.where(row_o < HEAD_DIM, acc_a[c] / l_a[c], acc_b[c] / l_b[c])
        o_ref[c * qs:(c + 1) * qs, :] = o_t.T


def _prompt_attention(qt, kb, vt, neg_f, *, batch, seq):
    tq = 512
    nq = seq // tq
    n_pairs = N_HEADS // 2
    return pl.pallas_call(
        functools.partial(_attn_kernel, tq=tq),
        grid=(batch, n_pairs, nq),
        in_specs=[pl.BlockSpec((LANES, tq), lambda b, p, i: (p, b * nq + i)),
                  pl.BlockSpec((seq, LANES), lambda b, p, i: (b, p)),
                  pl.BlockSpec((LANES, seq), lambda b, p, i: (p, b)),
                  pl.BlockSpec((None, 2, seq), lambda b, p, i: (b * n_pairs + p, 0, 0))],
        out_specs=pl.BlockSpec((tq, LANES), lambda b, p, i: (b * nq + i, p)),
        out_shape=jax.ShapeDtypeStruct(kb.shape, F32),
        compiler_params=_cparams(("arbitrary", "arbitrary", "arbitrary")),
        name="fox_prompt_attention",
    )(qt, kb, vt, neg_f)


def _ssd_prompt_kernel(xbc_ref, sa_ref, cw_ref, cb_ref, alog_row_ref, alog_col_ref, dexp_ref,
                       y_ref, ht_ref, ext_ref, hts_ref, *, ssd_w):
    c = pl.program_id(1)
    q_len = CHUNK
    tail = 8

    @pl.when(c == 0)
    def _():
        ext_ref[0:tail, :] = jnp.zeros((tail, ext_ref.shape[1]), F32)
        hts_ref[...] = jnp.zeros(hts_ref.shape, F32)

    x = xbc_ref[...]
    ext_ref[tail:tail + q_len, :] = x
    w = cw_ref[...]
    acc = cb_ref[...] + ext_ref[tail - 3:tail - 3 + q_len, :] * w[0:1]
    acc = acc + ext_ref[tail - 2:tail - 2 + q_len, :] * w[1:2]
    acc = acc + ext_ref[tail - 1:tail - 1 + q_len, :] * w[2:3]
    acc = acc + x * w[3:4]
    u = _silu(acc)
    ext_ref[0:tail, :] = x[q_len - tail:q_len, :]

    sa = sa_ref[...]
    lane = lax.broadcasted_iota(jnp.int32, (1, LANES), 1)
    dt_lanes = (lane >= SMALL_DT0) & (lane < SMALL_DT0 + N_HEADS)
    a_row = jnp.where(dt_lanes, -jnp.exp(alog_row_ref[...]), 0.0)
    lower = _tri((q_len, q_len), lambda r, c_: r >= c_)
    cs_full = _dot3_left(lower.astype(BF16), sa * a_row)
    dt_t = sa.T[SMALL_DT0:SMALL_DT0 + N_HEADS, :]
    da_t = dt_t * (-jnp.exp(alog_col_ref[...]))
    cs_t = _dot3_right(da_t, _tri((q_len, q_len), lambda r, c_: r <= c_).astype(BF16))
    tot_full = jnp.broadcast_to(cs_full[q_len - 1:q_len, :], cs_full.shape)

    lane_p = lax.broadcasted_iota(jnp.int32, (q_len, LANES), 1)
    first = lane_p < HEAD_DIM
    n_groups = N_HEADS // HEADS_PER_GROUP
    gn = n_groups * D_STATE
    for g in range(n_groups):
        bg = u[:, ssd_w + g * D_STATE: ssd_w + (g + 1) * D_STATE]
        cg = u[:, ssd_w + gn + g * D_STATE: ssd_w + gn + (g + 1) * D_STATE].astype(BF16)
        cb = lax.dot_general(cg, bg.astype(BF16), NT_DIMS, preferred_element_type=F32)
        bg_t = bg.T.astype(BF16)
        for k in range(HEADS_PER_GROUP // 2):
            pr = g * (HEADS_PER_GROUP // 2) + k
            x_pair = u[:, pr * LANES:(pr + 1) * LANES]
            x_pair_b = x_pair.astype(BF16)
            ys, cols, tots, dts = [], [], [], []
            for h in (2 * pr, 2 * pr + 1):
                col = cs_full[:, SMALL_DT0 + h:SMALL_DT0 + h + 1]
                seg = col - cs_t[h:h + 1, :]
                lmat = jnp.exp(jnp.where(lower, seg, -jnp.inf))
                mh = (cb * lmat * dt_t[h:h + 1, :]).astype(BF16)
                ys.append(_dot(mh, x_pair_b))
                cols.append(col)
                tots.append(tot_full[:, SMALL_DT0 + h:SMALL_DT0 + h + 1])
                dts.append(sa[:, SMALL_DT0 + h:SMALL_DT0 + h + 1])
            y_diag = jnp.where(first, ys[0], ys[1])
            h_prev = hts_ref[pr]
            e_col = jnp.where(first, jnp.exp(cols[0]), jnp.exp(cols[1]))
            y_off = _dot(cg, h_prev.astype(BF16)) * e_col
            y_ref[:, pr * LANES:(pr + 1) * LANES] = (
                y_diag + y_off + dexp_ref[:, pr * LANES:(pr + 1) * LANES] * x_pair)
            w_pair = jnp.where(first, jnp.exp(tots[0] - cols[0]) * dts[0], jnp.exp(tots[1] - cols[1]) * dts[1])
            new_t = _dot(bg_t, (x_pair * w_pair).astype(BF16))
            dec = jnp.where(first, jnp.exp(tots[0]), jnp.exp(tots[1]))
            hts_ref[pr] = h_prev * dec + new_t

    @pl.when(c == pl.num_programs(1) - 1)
    def _():
        ht_ref[...] = hts_ref[...]


def _ssd_prompt(xbc, small_act, conv_w, conv_b, alog_row, alog_col, d_exp, *, batch, seq):
    ssd_w = d_exp.shape[1]
    conv_dim = xbc.shape[1]
    nc = seq // CHUNK
    n_pairs = N_HEADS // 2
    full = lambda a: pl.BlockSpec(a.shape, lambda b, c: (0,) * a.ndim)
    return pl.pallas_call(
        functools.partial(_ssd_prompt_kernel, ssd_w=ssd_w),
        grid=(batch, nc),
        in_specs=[pl.BlockSpec((CHUNK, conv_dim), lambda b, c: (b * nc + c, 0)),
                  pl.BlockSpec((CHUNK, LANES), lambda b, c: (b * nc + c, 0)),
                  full(conv_w), full(conv_b), full(alog_row), full(alog_col), full(d_exp)],
        out_specs=[pl.BlockSpec((CHUNK, ssd_w), lambda b, c: (b * nc + c, 0)),
                   pl.BlockSpec((None, n_pairs, D_STATE, LANES), lambda b, c: (b, 0, 0, 0))],
        out_shape=[jax.ShapeDtypeStruct((batch * seq, ssd_w), F32),
                   jax.ShapeDtypeStruct((batch, n_pairs, D_STATE, LANES), F32)],
        scratch_shapes=[pltpu.VMEM((CHUNK + 8, conv_dim), F32),
                        pltpu.VMEM((n_pairs, D_STATE, LANES), F32)],
        compiler_params=_cparams(("arbitrary", "arbitrary")),
        name="ssd_prompt",
    )(xbc, small_act, conv_w, conv_b, alog_row, alog_col, d_exp)


def _merge_kernel(o_ref, y_ref, z_ref, x_ref, ga_ref, gatt_ref, gssd_ref, w_ref, out_ref, *, att_w):
    att = _rms(o_ref[...], gatt_ref[...]).astype(BF16)
    ssd = _rms(y_ref[...] * _silu(z_ref[...]), gssd_ref[...]).astype(BF16)
    mix = _dot(att, w_ref[0:att_w, :]) + _dot(ssd, w_ref[att_w:, :])
    out_ref[...] = x_ref[...] + ga_ref[...] * mix


def _mod_spec(tm, d, rows_per_mod):
    if rows_per_mod == 1:
        return pl.BlockSpec((tm, d), lambda i, *_: (i, 0))
    blocks_per_seq = rows_per_mod // tm
    return pl.BlockSpec((None, 1, d), lambda i, *_: (i // blocks_per_seq, 0, 0))


def _merge_outproj(o_att, y_ssd, z, x, ga1, g_att, g_ssd, w_out, *, tm, rows_per_mod):
    rows, d = x.shape
    att_w = o_att.shape[1]
    row = lambda w: pl.BlockSpec((tm, w), lambda i: (i, 0))
    full = lambda a: pl.BlockSpec(a.shape, lambda i: (0,) * a.ndim)
    return pl.pallas_call(
        functools.partial(_merge_kernel, att_w=att_w),
        grid=(rows // tm,),
        in_specs=[row(att_w), row(att_w), row(att_w), row(d), _mod_spec(tm, d, rows_per_mod),
                  full(g_att), full(g_ssd), full(w_out)],
        out_specs=row(d),
        out_shape=jax.ShapeDtypeStruct((rows, d), F32),
        compiler_params=_cparams(("arbitrary",)),
        name="merge_out_proj",
    )(o_att, y_ssd, z, x, ga1, g_att, g_ssd, w_out)


def _ffn_kernel(x_ref, sh_ref, sc_ref, ga_ref, g_ref, wg_ref, wu_ref, wd_ref, gf_ref, out_ref, h_ref, acc_ref):
    j = pl.program_id(1)

    @pl.when(j == 0)
    def _():
        h_ref[...] = (_rms(x_ref[...], g_ref[...]) * (1.0 + sc_ref[...]) + sh_ref[...]).astype(BF16)
        acc_ref[...] = jnp.zeros(acc_ref.shape, F32)

    h = h_ref[...]
    a = (_silu(_dot(h, wg_ref[...])) * _dot(h, wu_ref[...])).astype(BF16)
    acc_ref[...] += _dot(a, wd_ref[...])

    @pl.when(j == pl.num_programs(1) - 1)
    def _():
        out_ref[...] = _rms(x_ref[...] + ga_ref[...] * acc_ref[...], gf_ref[...])


def _ffn_final(x, sh, sc, ga, g_ffn, w_gate, w_up, w_down, g_final, *, tm, rows_per_mod):
    rows, d = x.shape
    hidden = w_gate.shape[1]
    th = hidden // 2
    mod = _mod_spec(tm, d, rows_per_mod)
    full = lambda a: pl.BlockSpec(a.shape, lambda i, j: (0,) * a.ndim)
    return pl.pallas_call(
        _ffn_kernel,
        grid=(rows // tm, hidden // th),
        in_specs=[pl.BlockSpec((tm, d), lambda i, j: (i, 0)), mod, mod, mod, full(g_ffn),
                  pl.BlockSpec((d, th), lambda i, j: (0, j)),
                  pl.BlockSpec((d, th), lambda i, j: (0, j)),
                  pl.BlockSpec((th, d), lambda i, j: (j, 0)),
                  full(g_final)],
        out_specs=pl.BlockSpec((tm, d), lambda i, j: (i, 0)),
        out_shape=jax.ShapeDtypeStruct((rows, d), F32),
        scratch_shapes=[pltpu.VMEM((tm, d), BF16), pltpu.VMEM((tm, d), F32)],
        compiler_params=_cparams(("arbitrary", "arbitrary")),
        name="ffn_final_norm",
    )(x, sh, sc, ga, g_ffn, w_gate, w_up, w_down, g_final)


def _page_suffix_kernel(x_ref, o_ref):
    x = x_ref[...]
    strict = _tri((LANES, LANES), lambda r, c: r > c).astype(BF16)
    o_ref[:, 0:LANES] = _dot3_right(x, strict)
    o_ref[:, LANES:2 * LANES] = _dot3_right(x, jnp.ones((LANES, LANES), BF16))


def _page_suffix(lf_t):
    rows = lf_t.shape[0]
    tr = 2048
    return pl.pallas_call(
        _page_suffix_kernel,
        grid=(rows // tr,),
        in_specs=[pl.BlockSpec((tr, LANES), lambda i: (i, 0))],
        out_specs=pl.BlockSpec((tr, 2 * LANES), lambda i: (i, 0)),
        out_shape=jax.ShapeDtypeStruct((rows, 2 * LANES), F32),
        compiler_params=_cparams(("arbitrary",)),
        name="page_logf_suffix",
    )(lf_t)


DECODE_PAGES = 32


def _decode_kernel(pt_ref, qbd_ref, kn_ref, vn_ref, cn_ref, *refs):
    n = DECODE_PAGES
    k_refs, v_refs, lf_refs = refs[0:n], refs[n:2 * n], refs[2 * n:3 * n]
    o_ref, m_ref, l_ref, acc_ref, carry_ref = refs[3 * n:]
    j = pl.program_id(1)
    att_w = N_HEADS * HEAD_DIM

    @pl.when(j == 0)
    def _():
        m_ref[...] = jnp.full(m_ref.shape, NEG_BIG, F32)
        l_ref[...] = jnp.zeros(l_ref.shape, F32)
        acc_ref[...] = jnp.zeros(acc_ref.shape, F32)
        carry_ref[...] = cn_ref[...]

    qbd = qbd_ref[...]
    carry = carry_ref[...]
    s_parts = [None] * n
    for i in reversed(range(n)):
        kt = k_refs[i][...].reshape(att_w, CHUNK).astype(BF16)
        s_parts[i] = _dot(qbd, kt) + (lf_refs[i][:, 0:CHUNK] + carry)
        carry = carry + lf_refs[i][:, CHUNK:2 * CHUNK]
    carry_ref[...] = carry
    s = jnp.concatenate(s_parts, axis=1)
    m_old = m_ref[...]
    m_new = jnp.maximum(m_old, jnp.max(s, axis=1, keepdims=True))
    alpha = jnp.exp(m_old - m_new)
    p = jnp.exp(s - m_new)
    l_ref[...] = alpha * l_ref[...] + jnp.sum(p, axis=1, keepdims=True)
    m_ref[...] = m_new
    pb = p.astype(BF16)
    pv = jnp.zeros(acc_ref.shape, F32)
    for i in range(n):
        vt = v_refs[i][...].reshape(att_w, CHUNK).astype(BF16)
        pv = pv + lax.dot_general(pb[:, i * CHUNK:(i + 1) * CHUNK], vt, NT_DIMS, preferred_element_type=F32)
    acc_ref[...] = alpha * acc_ref[...] + pv

    @pl.when(j == pl.num_programs(1) - 1)
    def _():
        s_new = jnp.sum(qbd.astype(F32) * kn_ref[...].astype(F32), axis=1, keepdims=True)
        m_fin = jnp.maximum(m_ref[...], s_new)
        a_fin = jnp.exp(m_ref[...] - m_fin)
        p_new = jnp.exp(s_new - m_fin)
        num = a_fin * acc_ref[...] + p_new.astype(BF16).astype(F32) * vn_ref[...].astype(F32)
        o8 = num / (a_fin * l_ref[...] + p_new)
        own = (lax.broadcasted_iota(jnp.int32, o8.shape, 1) // HEAD_DIM
               == lax.broadcasted_iota(jnp.int32, o8.shape, 0))
        o_ref[...] = jnp.sum(jnp.where(own, o8, 0.0), axis=0, keepdims=True)


def _decode_attention(page_table, qbd, kn, vn, cn_bc, cache_kt, cache_vt, lf_sums):
    dec_b, n_pages = page_table.shape
    n = DECODE_PAGES
    n_steps = n_pages // n
    att_w = N_HEADS * HEAD_DIM

    def per_b(shape):
        return pl.BlockSpec((None,) + shape, lambda b, j, pt: (b,) + (0,) * len(shape))

    def paged(shape, i):
        return pl.BlockSpec((None,) + shape,
                            lambda b, j, pt: (pt[b, (n_steps - 1 - j) * n + i],) + (0,) * len(shape))

    kv_specs = [paged((N_HEADS, HEAD_DIM, CHUNK), i) for i in range(n)]
    lf_specs = [paged((N_HEADS, 2 * CHUNK), i) for i in range(n)]
    return pl.pallas_call(
        _decode_kernel,
        grid_spec=pltpu.PrefetchScalarGridSpec(
            num_scalar_prefetch=1,
            grid=(dec_b, n_steps),
            in_specs=[per_b((N_HEADS, att_w)), per_b((1, att_w)), per_b((1, att_w)), per_b((N_HEADS, CHUNK))]
                     + kv_specs + kv_specs + lf_specs,
            out_specs=per_b((1, att_w)),
            scratch_shapes=[pltpu.VMEM((N_HEADS, 1), F32), pltpu.VMEM((N_HEADS, 1), F32),
                            pltpu.VMEM((N_HEADS, att_w), F32), pltpu.VMEM((N_HEADS, CHUNK), F32)]),
        out_shape=jax.ShapeDtypeStruct((dec_b, 1, att_w), F32),
        compiler_params=_cparams(("arbitrary", "arbitrary")),
        name="fox_decode_attention",
    )(page_table, qbd, kn, vn, cn_bc, *([cache_kt] * n), *([cache_vt] * n), *([lf_sums] * n))


def _sample_conv_kernel(s0_ref, s1_ref, s2_ref, x_ref, sa_ref, cw_ref, cb_ref, alog_row_ref, u_ref, dec_ref):
    w = cw_ref[...]
    acc = cb_ref[...] + s0_ref[...] * w[0:1]
    acc = acc + s1_ref[...] * w[1:2]
    acc = acc + s2_ref[...] * w[2:3]
    acc = acc + x_ref[...] * w[3:4]
    u_ref[...] = _silu(acc)
    dec_ref[...] = jnp.exp(sa_ref[...] * (-jnp.exp(alog_row_ref[...])))


def _sample_conv(s0, s1, s2, xbc, small_act, conv_w, conv_b, alog_row):
    return pl.pallas_call(
        _sample_conv_kernel,
        out_shape=[jax.ShapeDtypeStruct(xbc.shape, F32), jax.ShapeDtypeStruct(small_act.shape, F32)],
        name="sample_conv",
    )(s0, s1, s2, xbc, small_act, conv_w, conv_b, alog_row)


SSD_STEP_ROWS = 8


def _ssd_step_kernel(dt_ref, dec_ref, h_ref, xb_ref, b_ref, c_ref, dcol_ref, hn_ref, y_ref):
    n_groups = N_HEADS // HEADS_PER_GROUP
    for r in range(SSD_STEP_ROWS):
        b = pl.program_id(0) * SSD_STEP_ROWS + r
        xb = xb_ref[r]
        for h in range(N_HEADS):
            g = h // HEADS_PER_GROUP
            x_h = xb[h * HEAD_DIM:(h + 1) * HEAD_DIM, :]
            hn_ref[r, h] = dec_ref[b, h] * h_ref[r, h] + (dt_ref[b, h] * x_h) * b_ref[r, g:g + 1, :]
        hn_all = hn_ref[r].reshape(N_HEADS * HEAD_DIM, D_STATE).astype(BF16)
        yc = lax.dot_general(hn_all, c_ref[r].astype(BF16), NT_DIMS, preferred_element_type=F32)
        y_ref[r] = yc + dcol_ref[...] * xb[:, 0:yc.shape[1]]


def _ssd_step(dt, dec, state, xb, b_rows, c_rows, dcol):
    dec_b = state.shape[0]
    rows = SSD_STEP_ROWS
    smem = pl.BlockSpec(memory_space=pltpu.SMEM)
    blk = lambda a: pl.BlockSpec((rows,) + a.shape[1:], lambda b: (b,) + (0,) * (a.ndim - 1))
    y_shape = (dec_b, xb.shape[1], c_rows.shape[1])
    return pl.pallas_call(
        _ssd_step_kernel,
        grid=(dec_b // rows,),
        in_specs=[smem, smem, blk(state), blk(xb), blk(b_rows), blk(c_rows),
                  pl.BlockSpec(dcol.shape, lambda b: (0, 0))],
        out_specs=[blk(state), pl.BlockSpec((rows,) + y_shape[1:], lambda b: (b, 0, 0))],
        out_shape=[jax.ShapeDtypeStruct(state.shape, F32), jax.ShapeDtypeStruct(y_shape, F32)],
        compiler_params=_cparams(("arbitrary",)),
        name="ssd_step",
    )(dt, dec, state, xb, b_rows, c_rows, dcol)


def kernel(x_prompt, x_sample, c_prompt, c_sample, cache_k, cache_v, cache_logf, state_conv, state_ssm, page_table, w_ada, b_ada, g_mix, w_in, b_f, conv_w, conv_b, dt_bias, A_log, D_skip, g_att_out, g_ssd_out, w_out, g_ffn, w_gate, w_up, w_down, g_final):
    batch, seq, d = x_prompt.shape
    dec_b = x_sample.shape[0]
    att_w = N_HEADS * HEAD_DIM
    ssd_w = g_ssd_out.shape[1]
    conv_dim = conv_w.shape[2]
    n_pool = cache_k.shape[1]
    l = 0

    wi = w_in[l]
    o_f = 3 * att_w
    o_z = o_f + N_HEADS
    o_x = o_z + ssd_w
    o_dt = o_x + conv_dim
    w_big = jnp.concatenate([wi[:, :o_f], wi[:, o_z:o_dt]], axis=1).astype(BF16)
    pad = jnp.zeros((d, LANES - 2 * N_HEADS), F32)
    w_small = jnp.concatenate([wi[:, o_f:o_z], wi[:, o_dt:], pad], axis=1).astype(BF16)
    zpad = jnp.zeros((LANES - 2 * N_HEADS,), F32)
    bias_small = jnp.concatenate([b_f[l], dt_bias[l], zpad])[None, :]
    alog_row = jnp.concatenate([jnp.zeros((N_HEADS,), F32), A_log[l], zpad])[None, :]
    alog_col = A_log[l][:, None]
    d_exp = jnp.repeat(D_skip[l], HEAD_DIM)[None, :]
    w_out_b = w_out[l].astype(BF16)
    wg_b, wu_b, wd_b = w_gate[l].astype(BF16), w_up[l].astype(BF16), w_down[l].astype(BF16)
    g_final2 = g_final[None, :]

    ada_rows = 144
    c_all = jnp.concatenate([c_prompt, c_sample, jnp.zeros((ada_rows - batch - dec_b, d), F32)], axis=0)
    mod = _ada(c_all, w_ada[l], b_ada[l][None, :])
    mod_p = mod[:batch].reshape(batch, 6, 1, d)
    sh1_p, sc1_p, ga1_p, sh2_p, sc2_p, ga2_p = (mod_p[:, i] for i in range(6))
    mod_s = mod[batch:batch + dec_b].reshape(dec_b, 6, d)
    sh1_s, sc1_s, ga1_s, sh2_s, sc2_s, ga2_s = (mod_s[:, i] for i in range(6))

    xp = x_prompt.reshape(batch * seq, d)
    tm_p = 512
    _, k_p, v_p, kb, _, z_p, xbc_p, sa_p, qt, vt = _inproj(
        xp, sh1_p, sc1_p, g_mix[l][None, :], w_big, w_small, bias_small, tm=tm_p, rows_per_mod=seq)
    logf_p = sa_p[:, :N_HEADS].reshape(batch, seq, N_HEADS)
    neg_f = _neg_cumsum(jnp.swapaxes(logf_p, 1, 2).reshape(batch * N_HEADS, seq))
    o_att_p = _prompt_attention(qt, kb, vt, neg_f.reshape(batch * N_HEADS // 2, 2, seq), batch=batch, seq=seq)
    y_ssd_p, ht_p = _ssd_prompt(xbc_p, sa_p, conv_w[l], conv_b[l][None, :], alog_row, alog_col, d_exp,
                                batch=batch, seq=seq)
    x1_p = _merge_outproj(o_att_p, y_ssd_p, z_p, xp, ga1_p, g_att_out[l][None, :], g_ssd_out[l][None, :],
                          w_out_b, tm=tm_p, rows_per_mod=seq)
    y_p = _ffn_final(x1_p, sh2_p, sc2_p, ga2_p, g_ffn[l][None, :], wg_b, wu_b, wd_b, g_final2,
                     tm=tm_p, rows_per_mod=seq)

    xs = x_sample.reshape(dec_b, d)
    qs, k_s, v_s, ks_b, vs_b, z_s, xbc_s, sa_s, _, _ = _inproj(
        xs, sh1_s, sc1_s, g_mix[l][None, :], w_big, w_small, bias_small, tm=dec_b, rows_per_mod=1)
    lf_t = jnp.swapaxes(cache_logf[l], 1, 2).reshape(n_pool * N_HEADS, CHUNK)
    cache_kt = jnp.transpose(cache_k[l], (0, 2, 3, 1))
    cache_vt = jnp.transpose(cache_v[l], (0, 2, 3, 1))
    lf_sums = _page_suffix(lf_t)
    cn_bc = jnp.broadcast_to(sa_s[:, :N_HEADS, None], (dec_b, N_HEADS, CHUNK))
    own_head = jnp.eye(N_HEADS, dtype=bool)[None, :, :, None]
    qbd = jnp.where(own_head, qs.reshape(dec_b, 1, N_HEADS, HEAD_DIM), jnp.zeros((), BF16))
    o_att_s = _decode_attention(
        page_table, qbd.reshape(dec_b, N_HEADS, att_w), ks_b[:, None, :], vs_b[:, None, :], cn_bc,
        cache_kt, cache_vt,
        lf_sums.reshape(n_pool, N_HEADS, 2 * CHUNK)).reshape(dec_b, att_w)
    sc_l = state_conv[l]
    u_s, dec_s = _sample_conv(sc_l[:, 0], sc_l[:, 1], sc_l[:, 2], xbc_s, sa_s, conv_w[l], conv_b[l][None, :], alog_row)
    dt_s = sa_s[:, SMALL_DT0:SMALL_DT0 + N_HEADS]
    n_groups = N_HEADS // HEADS_PER_GROUP
    gn = n_groups * D_STATE
    xb = jnp.broadcast_to(u_s[:, :ssd_w, None], (dec_b, ssd_w, LANES))
    b_rows = u_s[:, ssd_w:ssd_w + gn].reshape(dec_b, n_groups, D_STATE)
    c_rows = jnp.pad(u_s[:, ssd_w + gn:].reshape(dec_b, n_groups, D_STATE), ((0, 0), (0, 8 - n_groups), (0, 0)))
    dcol = jnp.broadcast_to(d_exp[0][:, None], (ssd_w, 8))
    h_new, y8 = _ssd_step(dt_s, dec_s[:, SMALL_DT0:SMALL_DT0 + N_HEADS], state_ssm[l], xb, b_rows, c_rows, dcol)
    rows_per_group = ssd_w // n_groups
    y_ssd_s = jnp.concatenate([y8[:, g * rows_per_group:(g + 1) * rows_per_group, g] for g in range(n_groups)], axis=1)
    x1_s = _merge_outproj(o_att_s, y_ssd_s, z_s, xs, ga1_s, g_att_out[l][None, :],
                          g_ssd_out[l][None, :], w_out_b, tm=dec_b, rows_per_mod=1)
    y_s = _ffn_final(x1_s, sh2_s, sc2_s, ga2_s, g_ffn[l][None, :], wg_b, wu_b, wd_b, g_final2,
                     tm=dec_b, rows_per_mod=1)

    ht = ht_p.reshape(batch, N_HEADS // 2, D_STATE, 2, HEAD_DIM)
    ssm_prompt = jnp.transpose(ht, (0, 1, 3, 4, 2)).reshape(1, batch, N_HEADS, HEAD_DIM, D_STATE)
    conv_prompt = xbc_p.reshape(batch, seq, conv_dim)[:, seq - 3:, :][None]
    conv_sample = jnp.concatenate([sc_l[:, 1:], xbc_s[:, None, :]], axis=1)[None]
    return (y_p.reshape(batch, seq, d), y_s.reshape(dec_b, 1, d),
            k_p.reshape(1, batch, seq, N_HEADS, HEAD_DIM), v_p.reshape(1, batch, seq, N_HEADS, HEAD_DIM),
            logf_p[None],
            conv_prompt, ssm_prompt,
            k_s.reshape(1, dec_b, 1, N_HEADS, HEAD_DIM), v_s.reshape(1, dec_b, 1, N_HEADS, HEAD_DIM),
            sa_s[:, :N_HEADS].reshape(1, dec_b, 1, N_HEADS),
            conv_sample, h_new[None])
```
